```python
import math
import jax
import jax.numpy as jnp
from jax import lax
import numpy as np

D_MODEL = 2048
BATCH = 4
SEQ = 4096
DEPTH = 1

HEAD_DIM = 64
N_Q_HEADS = 16
N_KV_HEADS = 4
GQA_GROUP = N_Q_HEADS // N_KV_HEADS
ATTN_WIDTH = N_Q_HEADS * HEAD_DIM
KV_WIDTH = N_KV_HEADS * HEAD_DIM
WINDOW = 128
BLOCK = 128
NUM_BUCKETS = 32
MAX_DISTANCE = 128
NEG_INF = -1e30
SSM_WIDTH = D_MODEL // 4
SSM_GROUP_CH = 16
SSM_GROUPS = SSM_WIDTH // SSM_GROUP_CH
SSM_STATE = 64
D_FF = 4 * D_MODEL
N_BRANCHES = 2
IN_WIDTH = ATTN_WIDTH + 2 * KV_WIDTH + SSM_WIDTH + N_BRANCHES * D_MODEL
N_MOD = 6
EPS = 1e-6

kernel_name = "hybrid_swa_sink_s5_gated_adaln_block"


def _rmsnorm(x, g):
    xf = x.astype(jnp.float32)
    y = xf * lax.rsqrt(jnp.mean(xf * xf, axis=-1, keepdims=True) + EPS)
    return (y * g.astype(jnp.float32)).astype(x.dtype)


def _modulate(h, shift, scale):
    return h * (1 + scale[:, None, :]) + shift[:, None, :]


def _t5_buckets_block():
    qi = np.arange(BLOCK)[:, None]
    ki = np.arange(2 * BLOCK)[None, :]
    n = np.maximum(qi + BLOCK - ki, 0)
    max_exact = NUM_BUCKETS // 2
    large = max_exact + (np.log(np.maximum(n, 1) / max_exact)
                         / np.log(MAX_DISTANCE / max_exact)
                         * (NUM_BUCKETS - max_exact)).astype(np.int32)
    large = np.minimum(large, NUM_BUCKETS - 1)
    return np.where(n < max_exact, n, large).astype(np.int32)


def _sliding_window_attention(q, k, v, sinks, bias):
    b, s, _ = q.shape
    nb = s // BLOCK
    q = q.reshape(b, nb, BLOCK, N_KV_HEADS, GQA_GROUP, HEAD_DIM)
    pad = ((0, 0), (BLOCK, 0), (0, 0))
    kp = jnp.pad(k, pad).reshape(b, nb + 1, BLOCK, N_KV_HEADS, HEAD_DIM)
    vp = jnp.pad(v, pad).reshape(b, nb + 1, BLOCK, N_KV_HEADS, HEAD_DIM)
    kk = jnp.concatenate([kp[:, :-1], kp[:, 1:]], axis=2)
    vv = jnp.concatenate([vp[:, :-1], vp[:, 1:]], axis=2)
    scores = jnp.einsum('bnqhgd,bnkhd->bnhgqk', q, kk).astype(jnp.float32)
    scores = scores * (HEAD_DIM ** -0.5) + bias
    qi = jnp.arange(BLOCK)[:, None]
    ki = jnp.arange(2 * BLOCK)[None, :]
    dist = qi + BLOCK - ki
    band = (dist >= 0) & (dist < WINDOW)
    blk = jnp.arange(nb)[:, None, None]
    valid = band[None] & (blk * BLOCK + ki[None] - BLOCK >= 0)
    scores = jnp.where(valid[None, :, None, None], scores, NEG_INF)
    sink = sinks.astype(jnp.float32).reshape(N_KV_HEADS, GQA_GROUP, 1)
    m = jnp.maximum(jnp.max(scores, axis=-1), sink)
    p = jnp.exp(scores - m[..., None])
    denom = jnp.sum(p, axis=-1) + jnp.exp(sink - m)
    p = (p / denom[..., None]).astype(vv.dtype)
    o = jnp.einsum('bnhgqk,bnkhd->bnqhgd', p, vv)
    return o.reshape(b, s, ATTN_WIDTH)


def _ssm_combine(e1, e2):
    (a1r, a1i), (b1r, b1i) = e1
    (a2r, a2i), (b2r, b2i) = e2
    a_new = (a1r * a2r - a1i * a2i, a1r * a2i + a1i * a2r)
    b_new = (a2r * b1r - a2i * b1i + b2r, a2r * b1i + a2i * b1r + b2i)
    return (a_new, b_new)


def _s5_ssm(u, lambda_re, lambda_im, log_step, b_re, b_im, c_re, c_im, d_skip):
    bsz, s, _ = u.shape
    f32 = jnp.float32
    uf = u.astype(f32).reshape(bsz, s, SSM_GROUPS, SSM_GROUP_CH)
    lam_re = jnp.minimum(lambda_re.astype(f32), -1e-4)
    lam_im = lambda_im.astype(f32)
    delta = jnp.exp(log_step.astype(f32))[:, None]
    mag = jnp.exp(lam_re * delta)
    ang = lam_im * delta
    abar_re, abar_im = mag * jnp.cos(ang), mag * jnp.sin(ang)
    num_re, num_im = abar_re - 1.0, abar_im
    den = lam_re * lam_re + lam_im * lam_im
    f_re = (num_re * lam_re + num_im * lam_im) / den
    f_im = (num_im * lam_re - num_re * lam_im) / den
    br, bi = b_re.astype(f32), b_im.astype(f32)
    bbar_re = f_re[..., None] * br - f_im[..., None] * bi
    bbar_im = f_re[..., None] * bi + f_im[..., None] * br
    bu_re = jnp.einsum('bsgp,gnp->bsgn', uf, bbar_re)
    bu_im = jnp.einsum('bsgp,gnp->bsgn', uf, bbar_im)
    shape_a = (1, s, SSM_GROUPS, SSM_STATE)
    a_re = jnp.broadcast_to(abar_re, shape_a)
    a_im = jnp.broadcast_to(abar_im, shape_a)
    _, (x_re, x_im) = lax.associative_scan(
        _ssm_combine, ((a_re, a_im), (bu_re, bu_im)), axis=1)
    y = (jnp.einsum('bsgn,gpn->bsgp', x_re, c_re.astype(f32))
         - jnp.einsum('bsgn,gpn->bsgp', x_im, c_im.astype(f32))
         + d_skip.astype(f32).reshape(SSM_GROUPS, SSM_GROUP_CH) * uf)
    return y.reshape(bsz, s, SSM_WIDTH).astype(u.dtype)


def setup_inputs(seed: int = 0) -> dict:
    key = jax.random.key(seed)
    ks = jax.random.split(key, 32)

    def nrm(k, shape, scale):
        return jax.random.normal(k, shape, jnp.float32) * scale

    G, N, P = SSM_GROUPS, SSM_STATE, SSM_GROUP_CH
    lam_im0 = jnp.pi * jnp.arange(N, dtype=jnp.float32)
    return {
        "x": nrm(ks[0], (BATCH, SEQ, D_MODEL), 1.0),
        "c": nrm(ks[1], (BATCH, D_MODEL), 1.0),
        "w_ada": nrm(ks[2], (DEPTH, D_MODEL, N_MOD * D_MODEL), 0.5 * D_MODEL ** -0.5),
        "b_ada": nrm(ks[3], (DEPTH, N_MOD * D_MODEL), 0.02),
        "norm1_g": 1.0 + nrm(ks[4], (DEPTH, D_MODEL), 0.02),
        "w_in": nrm(ks[5], (DEPTH, D_MODEL, IN_WIDTH), D_MODEL ** -0.5),
        "b_in": nrm(ks[6], (DEPTH, IN_WIDTH), 0.02),
        "attn_sinks": nrm(ks[7], (DEPTH, N_Q_HEADS), 0.5),
        "rel_bias": nrm(ks[8], (NUM_BUCKETS, N_Q_HEADS), 0.1),
        "lambda_re": -0.5 + nrm(ks[9], (DEPTH, G, N), 0.01),
        "lambda_im": lam_im0 + nrm(ks[10], (DEPTH, G, N), 0.01),
        "log_step": jax.random.uniform(ks[11], (DEPTH, G), jnp.float32,
                                       minval=math.log(1e-3), maxval=math.log(1e-1)),
        "ssm_b_re": nrm(ks[12], (DEPTH, G, N, P), (2 * P) ** -0.5),
        "ssm_b_im": nrm(ks[13], (DEPTH, G, N, P), (2 * P) ** -0.5),
        "ssm_c_re": nrm(ks[14], (DEPTH, G, P, N), (2 * N) ** -0.5),
        "ssm_c_im": nrm(ks[15], (DEPTH, G, P, N), (2 * N) ** -0.5),
        "ssm_d": nrm(ks[16], (DEPTH, SSM_WIDTH), 1.0),
        "w_glu": nrm(ks[17], (DEPTH, SSM_WIDTH, SSM_WIDTH), SSM_WIDTH ** -0.5),
        "b_glu": nrm(ks[18], (DEPTH, SSM_WIDTH), 0.02),
        "w_attn_proj": nrm(ks[19], (DEPTH, ATTN_WIDTH, D_MODEL), ATTN_WIDTH ** -0.5),
        "w_ssm_proj": nrm(ks[20], (DEPTH, SSM_WIDTH, D_MODEL), SSM_WIDTH ** -0.5),
        "w_out": nrm(ks[21], (DEPTH, D_MODEL, D_MODEL), D_MODEL ** -0.5),
        "norm2_g": 1.0 + nrm(ks[22], (DEPTH, D_MODEL), 0.02),
        "w_ff1": nrm(ks[23], (DEPTH, D_MODEL, D_FF), D_MODEL ** -0.5),
        "w_ff2": nrm(ks[24], (DEPTH, D_FF, D_MODEL), D_FF ** -0.5),
        "final_g": 1.0 + nrm(ks[25], (D_MODEL,), 0.02),
    }


def reference(x, c, w_ada, b_ada, norm1_g, w_in, b_in, attn_sinks, rel_bias,
              lambda_re, lambda_im, log_step, ssm_b_re, ssm_b_im, ssm_c_re,
              ssm_c_im, ssm_d, w_glu, b_glu, w_attn_proj, w_ssm_proj, w_out,
              norm2_g, w_ff1, w_ff2, final_g):
    bsz, s, _ = x.shape
    buckets = jnp.asarray(_t5_buckets_block())
    bias = rel_bias.astype(jnp.float32)[buckets]
    bias = jnp.transpose(bias, (2, 0, 1)).reshape(N_KV_HEADS, GQA_GROUP, BLOCK, 2 * BLOCK)
    splits = [ATTN_WIDTH, ATTN_WIDTH + KV_WIDTH, ATTN_WIDTH + 2 * KV_WIDTH,
              ATTN_WIDTH + 2 * KV_WIDTH + SSM_WIDTH,
              ATTN_WIDTH + 2 * KV_WIDTH + SSM_WIDTH + D_MODEL]
    cs = jax.nn.silu(c)
    for l in range(DEPTH):
        mod = cs @ w_ada[l] + b_ada[l]
        sh1, sc1, g1, sh2, sc2, g2 = jnp.split(mod, N_MOD, axis=-1)
        h = _modulate(_rmsnorm(x, norm1_g[l]), sh1, sc1)
        proj = h @ w_in[l] + b_in[l]
        q, k, v, u, gate_a, gate_s = jnp.split(proj, splits, axis=-1)
        attn = _sliding_window_attention(q, k, v, attn_sinks[l], bias)
        y_attn = attn @ w_attn_proj[l]
        y = _s5_ssm(u, lambda_re[l], lambda_im[l], log_step[l], ssm_b_re[l],
                    ssm_b_im[l], ssm_c_re[l], ssm_c_im[l], ssm_d[l])
        z = jax.nn.gelu(y)
        z = z * jax.nn.sigmoid(z @ w_glu[l] + b_glu[l])
        y_ssm = z @ w_ssm_proj[l]
        merged = jax.nn.sigmoid(gate_a) * y_attn + jax.nn.sigmoid(gate_s) * y_ssm
        x = x + g1[:, None, :] * (merged @ w_out[l])
        h2 = _modulate(_rmsnorm(x, norm2_g[l]), sh2, sc2)
        ff = jnp.square(jax.nn.relu(h2 @ w_ff1[l])) @ w_ff2[l]
        x = x + g2[:, None, :] * ff
    return _rmsnorm(x, final_g)
```

```python
import functools
import math

import jax
import jax.numpy as jnp
import numpy as np
from jax import lax
from jax.experimental import pallas as pl
from jax.experimental.pallas import tpu as pltpu

F32 = jnp.float32
BF16 = jnp.bfloat16

D_MODEL = 2048
HEAD_DIM = 64
N_Q_HEADS = 16
N_KV_HEADS = 4
GQA_GROUP = N_Q_HEADS // N_KV_HEADS
ATTN_WIDTH = N_Q_HEADS * HEAD_DIM
KV_WIDTH = N_KV_HEADS * HEAD_DIM
WINDOW = 128
BLOCK = 128
NUM_BUCKETS = 32
MAX_DISTANCE = 128
NEG_INF = -1e30
SSM_WIDTH = D_MODEL // 4
SSM_GROUP_CH = 16
SSM_GROUPS = SSM_WIDTH // SSM_GROUP_CH
SSM_STATE = 64
SSM_LANES = SSM_GROUPS * SSM_STATE
D_FF = 4 * D_MODEL
IN_WIDTH = ATTN_WIDTH + 2 * KV_WIDTH + SSM_WIDTH + 2 * D_MODEL
N_MOD = 6
EPS = 1e-6

LANES = 128
SUBLANES = 8
VMEM_LIMIT = 56 * 1024 * 1024

SSM_SEG = 32
SSM_CHUNK = SUBLANES * SSM_SEG


def _t5_buckets_block():
    qi = np.arange(BLOCK)[:, None]
    ki = np.arange(2 * BLOCK)[None, :]
    n = np.maximum(qi + BLOCK - ki, 0)
    max_exact = NUM_BUCKETS // 2
    large = max_exact + (np.log(np.maximum(n, 1) / max_exact)
                         / np.log(MAX_DISTANCE / max_exact)
                         * (NUM_BUCKETS - max_exact)).astype(np.int32)
    large = np.minimum(large, NUM_BUCKETS - 1)
    return np.where(n < max_exact, n, large).astype(np.int32)


def _time_permutation():
    r = np.arange(SSM_CHUNK)
    tok = (r % SUBLANES) * SSM_SEG + r // SUBLANES
    p = np.zeros((SSM_CHUNK, SSM_CHUNK), np.float32)
    p[r, tok] = 1.0
    return p


def _adaln_kernel(c_ref, w_ref, b_ref, o_ref):
    c = c_ref[...]
    cs = (c * jax.nn.sigmoid(c)).astype(BF16)
    o_ref[...] = jnp.dot(cs, w_ref[...].astype(BF16),
                         preferred_element_type=F32) + b_ref[...]


def _adaln(c, w_ada, b_ada):
    bsz = c.shape[0]
    n = w_ada.shape[1]
    tn = 1024
    return pl.pallas_call(
        _adaln_kernel,
        grid=(n // tn,),
        in_specs=[pl.BlockSpec((bsz, D_MODEL), lambda j: (0, 0)),
                  pl.BlockSpec((D_MODEL, tn), lambda j: (0, j)),
                  pl.BlockSpec((1, tn), lambda j: (0, j))],
        out_specs=pl.BlockSpec((bsz, tn), lambda j: (0, j)),
        out_shape=jax.ShapeDtypeStruct((bsz, n), F32),
        compiler_params=pltpu.CompilerParams(
            dimension_semantics=("arbitrary",), vmem_limit_bytes=VMEM_LIMIT),
        name="adaln",
    )(c, w_ada, b_ada.reshape(1, n))


def _ssm_prep_kernel(lre_ref, lim_ref, ls_ref, btre_ref, btim_ref, ctre_ref,
                     ctim_ref, wbre_ref, wbim_ref, wcre_ref, wcim_ref,
                     are_ref, aim_ref, pre_ref, pim_ref, qre_ref, qim_ref):
    lam_re = jnp.minimum(lre_ref[...], -1e-4)
    lam_im = lim_ref[...]
    delta = jnp.exp(ls_ref[...])
    e_re = lam_re * delta
    e_im = lam_im * delta
    mag = jnp.exp(e_re)
    abar_re = mag * jnp.cos(e_im)
    abar_im = mag * jnp.sin(e_im)
    are_ref[...] = abar_re
    aim_ref[...] = abar_im
    num_re, num_im = abar_re - 1.0, abar_im
    den = lam_re * lam_re + lam_im * lam_im
    f_re = (num_re * lam_re + num_im * lam_im) / den
    f_im = (num_im * lam_re - num_re * lam_im) / den

    rg = lax.broadcasted_iota(jnp.int32, (SSM_WIDTH, SSM_LANES), 0) // SSM_GROUP_CH
    cg = lax.broadcasted_iota(jnp.int32, (SSM_WIDTH, SSM_LANES), 1) // SSM_STATE
    diag = rg == cg
    bt_re, bt_im = btre_ref[...], btim_ref[...]
    wbre_ref[...] = jnp.where(diag, f_re * bt_re - f_im * bt_im, 0.0).astype(BF16)
    wbim_ref[...] = jnp.where(diag, f_re * bt_im + f_im * bt_re, 0.0).astype(BF16)

    rg2 = lax.broadcasted_iota(jnp.int32, (SSM_LANES, SSM_WIDTH), 0) // SSM_STATE
    cg2 = lax.broadcasted_iota(jnp.int32, (SSM_LANES, SSM_WIDTH), 1) // SSM_GROUP_CH
    diag2 = rg2 == cg2
    wcre_ref[...] = jnp.where(diag2, ctre_ref[...], 0.0).astype(BF16)
    wcim_ref[...] = jnp.where(diag2, -ctim_ref[...], 0.0).astype(BF16)

    k = (lax.broadcasted_iota(jnp.int32, (SSM_SEG, SSM_LANES), 0) + 1).astype(F32)
    pmag = jnp.exp(k * e_re)
    pre_ref[...] = pmag * jnp.cos(k * e_im)
    pim_ref[...] = pmag * jnp.sin(k * e_im)
    row = lax.broadcasted_iota(jnp.int32, (SUBLANES, SSM_LANES), 0)
    kq = jnp.where(row == 0, 1.0, jnp.where(row == 1, 2.0, 4.0)) * float(SSM_SEG)
    qmag = jnp.exp(kq * e_re)
    qre_ref[...] = qmag * jnp.cos(kq * e_im)
    qim_ref[...] = qmag * jnp.sin(kq * e_im)


def _ssm_prep(lambda_re, lambda_im, log_step, b_re, b_im, c_re, c_im):
    g, n, p = SSM_GROUPS, SSM_STATE, SSM_GROUP_CH
    lre = lambda_re.reshape(1, g * n)
    lim = lambda_im.reshape(1, g * n)
    ls = jnp.broadcast_to(log_step[:, None], (g, n)).reshape(1, g * n)

    def bt(b):
        t = jnp.transpose(b, (0, 2, 1)).reshape(g * p, n)
        return jnp.tile(t, (1, g))

    def ct(c):
        t = jnp.transpose(c, (0, 2, 1)).reshape(g * n, p)
        return jnp.tile(t, (1, g))

    full = lambda shape: pl.BlockSpec(shape, lambda: (0,) * len(shape))
    outs = [((SSM_WIDTH, SSM_LANES), BF16), ((SSM_WIDTH, SSM_LANES), BF16),
            ((SSM_LANES, SSM_WIDTH), BF16), ((SSM_LANES, SSM_WIDTH), BF16),
            ((1, SSM_LANES), F32), ((1, SSM_LANES), F32),
            ((SSM_SEG, SSM_LANES), F32), ((SSM_SEG, SSM_LANES), F32),
            ((SUBLANES, SSM_LANES), F32), ((SUBLANES, SSM_LANES), F32)]
    ins = [lre, lim, ls, bt(b_re), bt(b_im), ct(c_re), ct(c_im)]
    return pl.pallas_call(
        _ssm_prep_kernel,
        in_specs=[full(a.shape) for a in ins],
        out_specs=[full(s) for s, _ in outs],
        out_shape=[jax.ShapeDtypeStruct(s, d) for s, d in outs],
        compiler_params=pltpu.CompilerParams(vmem_limit_bytes=VMEM_LIMIT),
        name="ssm_prep",
    )(*ins)


INPROJ_TN = 1024


def _inproj_kernel(x_ref, g_ref, sc_ref, sh_ref, w_ref, b_ref,
                   q_ref, kv_ref, u_ref, ga_ref, gs_ref, h_scr):
    j = pl.program_id(1)

    @pl.when(j == 0)
    def _():
        x = x_ref[...]
        ms = jnp.mean(x * x, axis=-1, keepdims=True)
        y = x * lax.rsqrt(ms + EPS)
        h = (y * g_ref[...]) * (1.0 + sc_ref[...]) + sh_ref[...]
        h_scr[...] = h.astype(BF16)

    acc = jnp.dot(h_scr[...], w_ref[...], preferred_element_type=F32) + b_ref[...]

    @pl.when(j == 0)
    def _():
        q_ref[...] = (acc * (HEAD_DIM ** -0.5)).astype(BF16)

    @pl.when(j == 1)
    def _():
        kv_ref[...] = acc[:, :2 * KV_WIDTH].astype(BF16)
        u_ref[...] = acc[:, 2 * KV_WIDTH:]

    @pl.when((j == 2) | (j == 3))
    def _():
        ga_ref[...] = jax.nn.sigmoid(acc).astype(BF16)

    @pl.when(j >= 4)
    def _():
        gs_ref[...] = jax.nn.sigmoid(acc).astype(BF16)


def _inproj(x2, g, sc, sh, w, b, seq, tm):
    rows = x2.shape[0]
    tn = INPROJ_TN
    tiles_per_seq = seq // tm
    bvec = lambda: pl.BlockSpec((None, 1, D_MODEL),
                                lambda i, j: (i // tiles_per_seq, 0, 0))
    return pl.pallas_call(
        _inproj_kernel,
        grid=(rows // tm, IN_WIDTH // tn),
        in_specs=[pl.BlockSpec((tm, D_MODEL), lambda i, j: (i, 0)),
                  pl.BlockSpec((1, D_MODEL), lambda i, j: (0, 0)),
                  bvec(), bvec(),
                  pl.BlockSpec((D_MODEL, tn), lambda i, j: (0, j)),
                  pl.BlockSpec((1, tn), lambda i, j: (0, j))],
        out_specs=[pl.BlockSpec((tm, ATTN_WIDTH), lambda i, j: (i, 0)),
                   pl.BlockSpec((tm, 2 * KV_WIDTH), lambda i, j: (i, 0)),
                   pl.BlockSpec((tm, SSM_WIDTH), lambda i, j: (i, 0)),
                   pl.BlockSpec((tm, tn), lambda i, j: (i, jnp.clip(j - 2, 0, 1))),
                   pl.BlockSpec((tm, tn), lambda i, j: (i, jnp.clip(j - 4, 0, 1)))],
        out_shape=[jax.ShapeDtypeStruct((rows, ATTN_WIDTH), BF16),
                   jax.ShapeDtypeStruct((rows, 2 * KV_WIDTH), BF16),
                   jax.ShapeDtypeStruct((rows, SSM_WIDTH), F32),
                   jax.ShapeDtypeStruct((rows, D_MODEL), BF16),
                   jax.ShapeDtypeStruct((rows, D_MODEL), BF16)],
        scratch_shapes=[pltpu.VMEM((tm, D_MODEL), BF16)],
        compiler_params=pltpu.CompilerParams(
            dimension_semantics=("arbitrary", "arbitrary"),
            vmem_limit_bytes=VMEM_LIMIT),
        name="inproj",
    )(x2, g, sc, sh, w, b)


ATTN_TQ = 512


def _attn_kernel(tiles_per_seq, sink_ref, q_ref, kvp_ref, kvc_ref, bias_ref, o_ref):
    first = (pl.program_id(0) % tiles_per_seq) == 0
    qi = lax.broadcasted_iota(jnp.int32, (BLOCK, 2 * BLOCK), 0)
    ki = lax.broadcasted_iota(jnp.int32, (BLOCK, 2 * BLOCK), 1)
    dist = qi + BLOCK - ki
    band = (dist >= 0) & (dist < WINDOW)
    band_first = band & (ki >= jnp.where(first, BLOCK, 0))
    for bi in range(ATTN_TQ // BLOCK):
        if bi == 0:
            kvblk = jnp.concatenate([kvp_ref[...], kvc_ref[0:BLOCK, :]], axis=0)
            valid = band_first
        else:
            kvblk = kvc_ref[(bi - 1) * BLOCK:(bi + 1) * BLOCK, :]
            valid = band
        for h in range(N_KV_HEADS):
            kh = kvblk[:, h * HEAD_DIM:(h + 1) * HEAD_DIM]
            vh = kvblk[:, KV_WIDTH + h * HEAD_DIM:KV_WIDTH + (h + 1) * HEAD_DIM]
            for gq in range(GQA_GROUP):
                a = h * GQA_GROUP + gq
                qa = q_ref[bi * BLOCK:(bi + 1) * BLOCK, a * HEAD_DIM:(a + 1) * HEAD_DIM]
                s = lax.dot_general(qa, kh, (((1,), (1,)), ((), ())),
                                    preferred_element_type=F32)
                s = jnp.where(valid, s + bias_ref[a], NEG_INF)
                sink = sink_ref[a]
                m = jnp.maximum(jnp.max(s, axis=-1, keepdims=True), sink)
                p = jnp.exp(s - m)
                denom = jnp.sum(p, axis=-1, keepdims=True) + jnp.exp(sink - m)
                o = jnp.dot(p.astype(BF16), vh, preferred_element_type=F32)
                o_ref[bi * BLOCK:(bi + 1) * BLOCK, a * HEAD_DIM:(a + 1) * HEAD_DIM] = (
                    o / denom).astype(BF16)


def _attention(q, kv, bias, sinks, seq):
    rows = q.shape[0]
    tq = ATTN_TQ
    blocks_per_tile = tq // BLOCK
    return pl.pallas_call(
        functools.partial(_attn_kernel, seq // tq),
        grid=(rows // tq,),
        in_specs=[pl.BlockSpec(memory_space=pltpu.SMEM),
                  pl.BlockSpec((tq, ATTN_WIDTH), lambda r: (r, 0)),
                  pl.BlockSpec((BLOCK, 2 * KV_WIDTH),
                               lambda r: (jnp.maximum(r * blocks_per_tile - 1, 0), 0)),
                  pl.BlockSpec((tq, 2 * KV_WIDTH), lambda r: (r, 0)),
                  pl.BlockSpec((N_Q_HEADS, BLOCK, 2 * BLOCK), lambda r: (0, 0, 0))],
        out_specs=pl.BlockSpec((tq, ATTN_WIDTH), lambda r: (r, 0)),
        out_shape=jax.ShapeDtypeStruct((rows, ATTN_WIDTH), BF16),
        compiler_params=pltpu.CompilerParams(
            dimension_semantics=("arbitrary",), vmem_limit_bytes=VMEM_LIMIT),
        name="attn",
    )(sinks, q, kv, kv, bias)


def _cmul(ar, ai, br, bi):
    return ar * br - ai * bi, ar * bi + ai * br


def _ssm_kernel(u_ref, perm_ref, permt_ref, wbre_ref, wbim_ref, wcre_ref, wcim_ref,
                are_ref, aim_ref, pre_ref, pim_ref, qre_ref, qim_ref, d_ref,
                wglu_ref, bglu_ref, o_ref,
                bre_scr, bim_scr, xb_scr, tre_scr, tim_scr):
    @pl.when(pl.program_id(1) == 0)
    def _():
        tre_scr[...] = jnp.zeros_like(tre_scr)
        tim_scr[...] = jnp.zeros_like(tim_scr)

    u = u_ref[...]
    u_hi = u.astype(BF16)
    u_lo = (u - u_hi.astype(F32)).astype(BF16)
    perm = perm_ref[...]
    up_hi = jnp.dot(perm, u_hi, preferred_element_type=F32)
    up = up_hi + jnp.dot(perm, u_lo, preferred_element_type=F32)
    ub = up_hi.astype(BF16)

    ct = 2 * LANES
    for t in range(SSM_LANES // ct):
        s = (t * ct // SSM_STATE * SSM_GROUP_CH) // LANES
        lhs = ub[:, s * LANES:(s + 1) * LANES]
        bre_scr[:, t * ct:(t + 1) * ct] = jnp.dot(
            lhs, wbre_ref[s * LANES:(s + 1) * LANES, t * ct:(t + 1) * ct],
            preferred_element_type=F32)
        bim_scr[:, t * ct:(t + 1) * ct] = jnp.dot(
            lhs, wbim_ref[s * LANES:(s + 1) * LANES, t * ct:(t + 1) * ct],
            preferred_element_type=F32)

    row = lax.broadcasted_iota(jnp.int32, (SUBLANES, LANES), 0)

    def lane_block(kb, carry):
        off = pl.multiple_of(kb * LANES, LANES)
        lanes = pl.ds(off, LANES)
        ar = jnp.broadcast_to(are_ref[:, lanes], (SUBLANES, LANES))
        ai = jnp.broadcast_to(aim_ref[:, lanes], (SUBLANES, LANES))
        hr = jnp.zeros((SUBLANES, LANES), F32)
        hi = jnp.zeros((SUBLANES, LANES), F32)
        for i in range(SSM_SEG):
            rows = pl.ds(i * SUBLANES, SUBLANES)
            mr, mi = _cmul(ar, ai, hr, hi)
            hr = mr + bre_scr[rows, lanes]
            hi = mi + bim_scr[rows, lanes]
            bre_scr[rows, lanes] = hr
            bim_scr[rows, lanes] = hi
        dr = pltpu.roll(jnp.where(row == SUBLANES - 1, tre_scr[:, lanes], hr), 1, 0)
        di = pltpu.roll(jnp.where(row == SUBLANES - 1, tim_scr[:, lanes], hi), 1, 0)
        for lvl, sh in enumerate((1, 2, 4)):
            qr = jnp.broadcast_to(qre_ref[pl.ds(lvl, 1), lanes], (SUBLANES, LANES))
            qi = jnp.broadcast_to(qim_ref[pl.ds(lvl, 1), lanes], (SUBLANES, LANES))
            sr = jnp.where(row >= sh, pltpu.roll(dr, sh, 0), 0.0)
            si = jnp.where(row >= sh, pltpu.roll(di, sh, 0), 0.0)
            mr, mi = _cmul(qr, qi, sr, si)
            dr, di = dr + mr, di + mi
        q1r = jnp.broadcast_to(qre_ref[pl.ds(0, 1), lanes], (SUBLANES, LANES))
        q1i = jnp.broadcast_to(qim_ref[pl.ds(0, 1), lanes], (SUBLANES, LANES))
        mr, mi = _cmul(q1r, q1i, dr, di)
        tre_scr[:, lanes] = mr + hr
        tim_scr[:, lanes] = mi + hi
        for i2 in range(SSM_SEG // 2):
            parts_r, parts_i = [], []
            for i in (2 * i2, 2 * i2 + 1):
                rows = pl.ds(i * SUBLANES, SUBLANES)
                pr = jnp.broadcast_to(pre_ref[pl.ds(i, 1), lanes], (SUBLANES, LANES))
                pi = jnp.broadcast_to(pim_ref[pl.ds(i, 1), lanes], (SUBLANES, LANES))
                mr, mi = _cmul(pr, pi, dr, di)
                parts_r.append(bre_scr[rows, lanes] + mr)
                parts_i.append(bim_scr[rows, lanes] + mi)
            rows2 = pl.ds(i2 * 2 * SUBLANES, 2 * SUBLANES)
            xb_scr[rows2, lanes] = jnp.concatenate(parts_r, axis=0).astype(BF16)
            xb_scr[rows2, pl.ds(pl.multiple_of(SSM_LANES + off, LANES), LANES)] = (
                jnp.concatenate(parts_i, axis=0).astype(BF16))
        return carry

    lax.fori_loop(0, SSM_LANES // LANES, lane_block, 0)

    kc = SSM_LANES // (SSM_WIDTH // LANES)
    ys = []
    for m in range(SSM_WIDTH // LANES):
        cols = slice(m * LANES, (m + 1) * LANES)
        yre = jnp.dot(xb_scr[:, m * kc:(m + 1) * kc], wcre_ref[m * kc:(m + 1) * kc, cols],
                      preferred_element_type=F32)
        yim = jnp.dot(xb_scr[:, SSM_LANES + m * kc:SSM_LANES + (m + 1) * kc],
                      wcim_ref[m * kc:(m + 1) * kc, cols], preferred_element_type=F32)
        ys.append(yre + yim)
    y = jnp.concatenate(ys, axis=1) + d_ref[...] * up

    z = jax.nn.gelu(y)
    gl = jnp.dot(z.astype(BF16), wglu_ref[...], preferred_element_type=F32) + bglu_ref[...]
    zg = (z * jax.nn.sigmoid(gl)).astype(BF16)
    o_ref[...] = jnp.dot(permt_ref[...], zg, preferred_element_type=F32).astype(BF16)


def _ssm(u, prep, ssm_d, w_glu, b_glu, bsz, seq):
    wbre, wbim, wcre, wcim, are, aim, pre, pim, qre, qim = prep
    perm = _time_permutation()
    chunks = seq // SSM_CHUNK
    const = lambda a: pl.BlockSpec(a.shape, lambda b, c: (0,) * a.ndim)
    consts = [jnp.asarray(perm, BF16), jnp.asarray(perm.T, BF16), wbre, wbim, wcre,
              wcim, are, aim, pre, pim, qre, qim, ssm_d.reshape(1, SSM_WIDTH),
              w_glu, b_glu.reshape(1, SSM_WIDTH)]
    return pl.pallas_call(
        _ssm_kernel,
        grid=(bsz, chunks),
        in_specs=[pl.BlockSpec((SSM_CHUNK, SSM_WIDTH), lambda b, c: (b * chunks + c, 0))]
        + [const(a) for a in consts],
        out_specs=pl.BlockSpec((SSM_CHUNK, SSM_WIDTH), lambda b, c: (b * chunks + c, 0)),
        out_shape=jax.ShapeDtypeStruct((bsz * seq, SSM_WIDTH), BF16),
        scratch_shapes=[pltpu.VMEM((SSM_CHUNK, SSM_LANES), F32),
                        pltpu.VMEM((SSM_CHUNK, SSM_LANES), F32),
                        pltpu.VMEM((SSM_CHUNK, 2 * SSM_LANES), BF16),
                        pltpu.VMEM((SUBLANES, SSM_LANES), F32),
                        pltpu.VMEM((SUBLANES, SSM_LANES), F32)],
        compiler_params=pltpu.CompilerParams(
            dimension_semantics=("arbitrary", "arbitrary"),
            vmem_limit_bytes=VMEM_LIMIT),
        name="ssm",
    )(u, *consts)


def _merge_kernel(attn_ref, zg_ref, ga_ref, gs_ref, x_ref, g1_ref,
                  wap_ref, wsp_ref, wout_ref, o_ref):
    ya = jnp.dot(attn_ref[...], wap_ref[...], preferred_element_type=F32)
    ys = jnp.dot(zg_ref[...], wsp_ref[...], preferred_element_type=F32)
    merged = ga_ref[...].astype(F32) * ya + gs_ref[...].astype(F32) * ys
    o = jnp.dot(merged.astype(BF16), wout_ref[...], preferred_element_type=F32)
    o_ref[...] = x_ref[...] + g1_ref[...] * o


def _merge(attn, zg, ga, gs, x2, g1, wap, wsp, wout, seq, tm):
    rows = x2.shape[0]
    tiles_per_seq = seq // tm
    row = lambda w: pl.BlockSpec((tm, w), lambda i: (i, 0))
    const = lambda a: pl.BlockSpec(a.shape, lambda i: (0, 0),
                                   pipeline_mode=pl.Buffered(1))
    return pl.pallas_call(
        _merge_kernel,
        grid=(rows // tm,),
        in_specs=[row(ATTN_WIDTH), row(SSM_WIDTH), row(D_MODEL), row(D_MODEL),
                  row(D_MODEL),
                  pl.BlockSpec((None, 1, D_MODEL), lambda i: (i // tiles_per_seq, 0, 0)),
                  const(wap), const(wsp), const(wout)],
        out_specs=row(D_MODEL),
        out_shape=jax.ShapeDtypeStruct((rows, D_MODEL), F32),
        compiler_params=pltpu.CompilerParams(
            dimension_semantics=("arbitrary",), vmem_limit_bytes=VMEM_LIMIT),
        name="merge",
    )(attn, zg, ga, gs, x2, g1, wap, wsp, wout)


def _ffn_kernel(x_ref, g_ref, sc_ref, sh_ref, g2_ref, fg_ref, w1_ref, w2_ref,
                o_ref, h_scr, acc_scr):
    j = pl.program_id(1)

    @pl.when(j == 0)
    def _():
        x = x_ref[...]
        ms = jnp.mean(x * x, axis=-1, keepdims=True)
        y = x * lax.rsqrt(ms + EPS)
        h = (y * g_ref[...]) * (1.0 + sc_ref[...]) + sh_ref[...]
        h_scr[...] = h.astype(BF16)
        acc_scr[...] = jnp.zeros_like(acc_scr)

    a = jnp.dot(h_scr[...], w1_ref[...], preferred_element_type=F32)
    a = jnp.square(jnp.maximum(a, 0.0)).astype(BF16)
    acc_scr[...] += jnp.dot(a, w2_ref[...], preferred_element_type=F32)

    @pl.when(j == pl.num_programs(1) - 1)
    def _():
        xo = x_ref[...] + g2_ref[...] * acc_scr[...]
        ms = jnp.mean(xo * xo, axis=-1, keepdims=True)
        o_ref[...] = (xo * lax.rsqrt(ms + EPS)) * fg_ref[...]


def _ffn(x1, g, sc, sh, g2, fg, w1, w2, seq, tm, tf):
    rows = x1.shape[0]
    tiles_per_seq = seq // tm
    bvec = lambda: pl.BlockSpec((None, 1, D_MODEL),
                                lambda i, j: (i // tiles_per_seq, 0, 0))
    vec = lambda: pl.BlockSpec((1, D_MODEL), lambda i, j: (0, 0))
    return pl.pallas_call(
        _ffn_kernel,
        grid=(rows // tm, D_FF // tf),
        in_specs=[pl.BlockSpec((tm, D_MODEL), lambda i, j: (i, 0)),
                  vec(), bvec(), bvec(), bvec(), vec(),
                  pl.BlockSpec((D_MODEL, tf), lambda i, j: (0, j)),
                  pl.BlockSpec((tf, D_MODEL), lambda i, j: (j, 0))],
        out_specs=pl.BlockSpec((tm, D_MODEL), lambda i, j: (i, 0)),
        out_shape=jax.ShapeDtypeStruct((rows, D_MODEL), F32),
        scratch_shapes=[pltpu.VMEM((tm, D_MODEL), BF16),
                        pltpu.VMEM((tm, D_MODEL), F32)],
        compiler_params=pltpu.CompilerParams(
            dimension_semantics=("arbitrary", "arbitrary"),
            vmem_limit_bytes=VMEM_LIMIT),
        name="ffn",
    )(x1, g, sc, sh, g2, fg, w1, w2)


def kernel(x, c, w_ada, b_ada, norm1_g, w_in, b_in, attn_sinks, rel_bias, lambda_re,
           lambda_im, log_step, ssm_b_re, ssm_b_im, ssm_c_re, ssm_c_im, ssm_d, w_glu,
           b_glu, w_attn_proj, w_ssm_proj, w_out, norm2_g, w_ff1, w_ff2, final_g):
    bsz, seq, _ = x.shape
    depth = w_ada.shape[0]
    assert depth == 1, "the fused final norm assumes a single layer"
    rows = bsz * seq

    buckets = jnp.asarray(_t5_buckets_block())
    bias = jnp.transpose(rel_bias.astype(F32)[buckets], (2, 0, 1))

    x2 = x.reshape(rows, D_MODEL)
    for l in range(depth):
        mod = _adaln(c, w_ada[l], b_ada[l])
        sh1, sc1, g1, sh2, sc2, g2 = [
            m.reshape(bsz, 1, D_MODEL) for m in jnp.split(mod, N_MOD, axis=-1)]
        prep = _ssm_prep(lambda_re[l], lambda_im[l], log_step[l], ssm_b_re[l],
                         ssm_b_im[l], ssm_c_re[l], ssm_c_im[l])

        q, kv, u, ga, gs = _inproj(
            x2, norm1_g[l].reshape(1, D_MODEL), sc1, sh1, w_in[l].astype(BF16),
            b_in[l].reshape(1, IN_WIDTH), seq, tm=512)
        attn = _attention(q, kv, bias, attn_sinks[l], seq)
        zg = _ssm(u, prep, ssm_d[l], w_glu[l].astype(BF16), b_glu[l], bsz, seq)
        x2 = _merge(attn, zg, ga, gs, x2, g1, w_attn_proj[l].astype(BF16),
                    w_ssm_proj[l].astype(BF16), w_out[l].astype(BF16), seq, tm=512)
        x2 = _ffn(x2, norm2_g[l].reshape(1, D_MODEL), sc2, sh2, g2,
                  final_g.reshape(1, D_MODEL), w_ff1[l].astype(BF16), w_ff2[l].astype(BF16),
                  seq, tm=512, tf=1024)
    return x2.reshape(bsz, seq, D_MODEL)
```

```python
import functools
import math

import jax
import jax.numpy as jnp
import numpy as np
from jax import lax
from jax.experimental import pallas as pl
from jax.experimental.pallas import tpu as pltpu

F32 = jnp.float32
BF16 = jnp.bfloat16

D_MODEL = 2048
HEAD_DIM = 64
N_Q_HEADS = 16
N_KV_HEADS = 4
GQA_GROUP = N_Q_HEADS // N_KV_HEADS
ATTN_WIDTH = N_Q_HEADS * HEAD_DIM
KV_WIDTH = N_KV_HEADS * HEAD_DIM
WINDOW = 128
BLOCK = 128
NUM_BUCKETS = 32
MAX_DISTANCE = 128
NEG_INF = -1e30
SSM_WIDTH = D_MODEL // 4
SSM_GROUP_CH = 16
SSM_GROUPS = SSM_WIDTH // SSM_GROUP_CH
SSM_STATE = 64
SSM_LANES = SSM_GROUPS * SSM_STATE
D_FF = 4 * D_MODEL
IN_WIDTH = ATTN_WIDTH + 2 * KV_WIDTH + SSM_WIDTH + 2 * D_MODEL
N_MOD = 6
EPS = 1e-6

LANES = 128
SUBLANES = 8
VMEM_LIMIT = 56 * 1024 * 1024

SSM_SEG = 32
SSM_CHUNK = SUBLANES * SSM_SEG


def _time_permutation():
    r = np.arange(SSM_CHUNK)
    tok = (r % SUBLANES) * SSM_SEG + r // SUBLANES
    p = np.zeros((SSM_CHUNK, SSM_CHUNK), np.float32)
    p[r, tok] = 1.0
    return p


def _adaln_kernel(c_ref, w_ref, b_ref, o_ref):
    c = c_ref[...]
    cs = (c * jax.nn.sigmoid(c)).astype(BF16)
    o_ref[...] = jnp.dot(cs, w_ref[...].astype(BF16),
                         preferred_element_type=F32) + b_ref[...]


def _adaln(c, w_ada, b_ada):
    bsz = c.shape[0]
    n = w_ada.shape[1]
    tn = 1024
    return pl.pallas_call(
        _adaln_kernel,
        grid=(n // tn,),
        in_specs=[pl.BlockSpec((bsz, D_MODEL), lambda j: (0, 0)),
                  pl.BlockSpec((D_MODEL, tn), lambda j: (0, j)),
                  pl.BlockSpec((1, tn), lambda j: (0, j))],
        out_specs=pl.BlockSpec((bsz, tn), lambda j: (0, j)),
        out_shape=jax.ShapeDtypeStruct((bsz, n), F32),
        compiler_params=pltpu.CompilerParams(
            dimension_semantics=("arbitrary",), vmem_limit_bytes=VMEM_LIMIT),
        name="adaln",
    )(c, w_ada, b_ada.reshape(1, n))


def _ssm_prep_kernel(lre_ref, lim_ref, ls_ref, btre_ref, btim_ref, ctre_ref,
                     ctim_ref, wbre_ref, wbim_ref, wcre_ref, wcim_ref,
                     are_ref, aim_ref, pre_ref, pim_ref, qre_ref, qim_ref):
    lam_re = jnp.minimum(lre_ref[...], -1e-4)
    lam_im = lim_ref[...]
    delta = jnp.exp(ls_ref[...])
    e_re = lam_re * delta
    e_im = lam_im * delta
    mag = jnp.exp(e_re)
    abar_re = mag * jnp.cos(e_im)
    abar_im = mag * jnp.sin(e_im)
    are_ref[...] = abar_re
    aim_ref[...] = abar_im
    num_re, num_im = abar_re - 1.0, abar_im
    den = lam_re * lam_re + lam_im * lam_im
    f_re = (num_re * lam_re + num_im * lam_im) / den
    f_im = (num_im * lam_re - num_re * lam_im) / den

    rg = lax.broadcasted_iota(jnp.int32, (SSM_WIDTH, SSM_LANES), 0) // SSM_GROUP_CH
    cg = lax.broadcasted_iota(jnp.int32, (SSM_WIDTH, SSM_LANES), 1) // SSM_STATE
    diag = rg == cg
    bt_re, bt_im = btre_ref[...], btim_ref[...]
    wbre_ref[...] = jnp.where(diag, f_re * bt_re - f_im * bt_im, 0.0).astype(BF16)
    wbim_ref[...] = jnp.where(diag, f_re * bt_im + f_im * bt_re, 0.0).astype(BF16)

    rg2 = lax.broadcasted_iota(jnp.int32, (SSM_LANES, SSM_WIDTH), 0) // SSM_STATE
    cg2 = lax.broadcasted_iota(jnp.int32, (SSM_LANES, SSM_WIDTH), 1) // SSM_GROUP_CH
    diag2 = rg2 == cg2
    wcre_ref[...] = jnp.where(diag2, ctre_ref[...], 0.0).astype(BF16)
    wcim_ref[...] = jnp.where(diag2, -ctim_ref[...], 0.0).astype(BF16)

    k = (lax.broadcasted_iota(jnp.int32, (SSM_SEG, SSM_LANES), 0) + 1).astype(F32)
    pmag = jnp.exp(k * e_re)
    pre_ref[...] = pmag * jnp.cos(k * e_im)
    pim_ref[...] = pmag * jnp.sin(k * e_im)
    row = lax.broadcasted_iota(jnp.int32, (SUBLANES, SSM_LANES), 0)
    kq = jnp.where(row == 0, 1.0, jnp.where(row == 1, 2.0, 4.0)) * float(SSM_SEG)
    qmag = jnp.exp(kq * e_re)
    qre_ref[...] = qmag * jnp.cos(kq * e_im)
    qim_ref[...] = qmag * jnp.sin(kq * e_im)


def _ssm_prep(lambda_re, lambda_im, log_step, b_re, b_im, c_re, c_im):
    g, n, p = SSM_GROUPS, SSM_STATE, SSM_GROUP_CH
    lre = lambda_re.reshape(1, g * n)
    lim = lambda_im.reshape(1, g * n)
    ls = jnp.broadcast_to(log_step[:, None], (g, n)).reshape(1, g * n)

    def bt(b):
        t = jnp.transpose(b, (0, 2, 1)).reshape(g * p, n)
        return jnp.tile(t, (1, g))

    def ct(c):
        t = jnp.transpose(c, (0, 2, 1)).reshape(g * n, p)
        return jnp.tile(t, (1, g))

    full = lambda shape: pl.BlockSpec(shape, lambda: (0,) * len(shape))
    outs = [((SSM_WIDTH, SSM_LANES), BF16), ((SSM_WIDTH, SSM_LANES), BF16),
            ((SSM_LANES, SSM_WIDTH), BF16), ((SSM_LANES, SSM_WIDTH), BF16),
            ((1, SSM_LANES), F32), ((1, SSM_LANES), F32),
            ((SSM_SEG, SSM_LANES), F32), ((SSM_SEG, SSM_LANES), F32),
            ((SUBLANES, SSM_LANES), F32), ((SUBLANES, SSM_LANES), F32)]
    ins = [lre, lim, ls, bt(b_re), bt(b_im), ct(c_re), ct(c_im)]
    return pl.pallas_call(
        _ssm_prep_kernel,
        in_specs=[full(a.shape) for a in ins],
        out_specs=[full(s) for s, _ in outs],
        out_shape=[jax.ShapeDtypeStruct(s, d) for s, d in outs],
        compiler_params=pltpu.CompilerParams(vmem_limit_bytes=VMEM_LIMIT),
        name="ssm_prep",
    )(*ins)


INPROJ_TM = 512
INPROJ_SUB = 256
INPROJ_CHUNK = 512
Q_END = ATTN_WIDTH
KV_END = Q_END + 2 * KV_WIDTH
U_END = KV_END + SSM_WIDTH
GA_END = U_END + D_MODEL


def _norm_modulate(x, g, sc, sh):
    ms = jnp.mean(x * x, axis=-1, keepdims=True)
    y = x * lax.rsqrt(ms + EPS)
    return ((y * g) * (1.0 + sc) + sh).astype(BF16)


def _inproj_kernel(x_ref, g_ref, sc_ref, sh_ref, w_ref, b_ref,
                   q_ref, kv_ref, u_ref, ga_ref, gs_ref, h_scr):
    g, sc, sh = g_ref[...], sc_ref[...], sh_ref[...]
    for r0 in range(0, INPROJ_TM, INPROJ_SUB):
        rows = slice(r0, r0 + INPROJ_SUB)
        h_scr[rows, :] = _norm_modulate(x_ref[rows, :], g, sc, sh)
    for r0 in range(0, INPROJ_TM, INPROJ_SUB):
        rows = slice(r0, r0 + INPROJ_SUB)
        h = h_scr[rows, :]
        for c0 in range(0, IN_WIDTH, INPROJ_CHUNK):
            cols = slice(c0, c0 + INPROJ_CHUNK)
            acc = jnp.dot(h, w_ref[:, cols], preferred_element_type=F32) + b_ref[:, cols]
            if c0 < Q_END:
                q_ref[rows, cols] = (acc * (HEAD_DIM ** -0.5)).astype(BF16)
            elif c0 < KV_END:
                kv_ref[rows, c0 - Q_END:c0 - Q_END + INPROJ_CHUNK] = acc.astype(BF16)
            elif c0 < U_END:
                u_ref[rows, c0 - KV_END:c0 - KV_END + INPROJ_CHUNK] = acc
            elif c0 < GA_END:
                ga_ref[rows, c0 - U_END:c0 - U_END + INPROJ_CHUNK] = (
                    jax.nn.sigmoid(acc).astype(BF16))
            else:
                gs_ref[rows, c0 - GA_END:c0 - GA_END + INPROJ_CHUNK] = (
                    jax.nn.sigmoid(acc).astype(BF16))


def _inproj(x2, g, sc, sh, w, b, seq):
    rows = x2.shape[0]
    tm = INPROJ_TM
    tiles_per_seq = seq // tm
    bvec = lambda: pl.BlockSpec((None, 1, D_MODEL), lambda i: (i // tiles_per_seq, 0, 0))
    const = lambda a: pl.BlockSpec(a.shape, lambda i: (0, 0), pipeline_mode=pl.Buffered(1))
    row = lambda width: pl.BlockSpec((tm, width), lambda i: (i, 0))
    return pl.pallas_call(
        _inproj_kernel,
        grid=(rows // tm,),
        in_specs=[row(D_MODEL), const(g), bvec(), bvec(), const(w), const(b)],
        out_specs=[row(ATTN_WIDTH), row(2 * KV_WIDTH), row(SSM_WIDTH),
                   row(D_MODEL), row(D_MODEL)],
        out_shape=[jax.ShapeDtypeStruct((rows, ATTN_WIDTH), BF16),
                   jax.ShapeDtypeStruct((rows, 2 * KV_WIDTH), BF16),
                   jax.ShapeDtypeStruct((rows, SSM_WIDTH), F32),
                   jax.ShapeDtypeStruct((rows, D_MODEL), BF16),
                   jax.ShapeDtypeStruct((rows, D_MODEL), BF16)],
        scratch_shapes=[pltpu.VMEM((tm, D_MODEL), BF16)],
        compiler_params=pltpu.CompilerParams(
            dimension_semantics=("arbitrary",), vmem_limit_bytes=VMEM_LIMIT),
        name="inproj",
    )(x2, g, sc, sh, w, b)


ATTN_TQ = 512
KV_PAIRS = N_KV_HEADS // 2
HEADS_PER_PAIR = 2 * GQA_GROUP


def _paired_head(o, half, m):
    return (2 * o + half) * GQA_GROUP + m


def _to_paired_cols(w):
    lead = w.shape[:-1]
    w = w.reshape(*lead, KV_PAIRS, 2, GQA_GROUP, HEAD_DIM)
    return jnp.swapaxes(w, -3, -2).reshape(*lead, ATTN_WIDTH)


def _t5_bucket_of_distance(n):
    max_exact = NUM_BUCKETS // 2
    large = max_exact + (np.log(np.maximum(n, 1) / max_exact)
                         / np.log(MAX_DISTANCE / max_exact)
                         * (NUM_BUCKETS - max_exact)).astype(np.int32)
    large = np.minimum(large, NUM_BUCKETS - 1)
    return np.where(n < max_exact, n, large).astype(np.int32)


def _bias_tables(rel_bias):
    val = rel_bias.astype(F32)[_t5_bucket_of_distance(np.arange(WINDOW))].T
    pad = jnp.full((N_Q_HEADS, BLOCK), NEG_INF, F32)
    period = 3 * BLOCK
    f = jnp.concatenate([pad, val[:, ::-1], pad], axis=1)
    g = jnp.roll(f, -(BLOCK - 1), axis=1)
    flat = jnp.tile(g, (1, BLOCK))[:, :BLOCK * (period - 1)]
    tab = flat.reshape(N_Q_HEADS, BLOCK, period - 1)[:, :, :2 * BLOCK]
    ki = np.arange(2 * BLOCK)[None, None, :]
    tab_first = jnp.where(ki >= BLOCK, tab, NEG_INF)
    order = [_paired_head(o, half, m) for o in range(KV_PAIRS)
             for half in range(2) for m in range(GQA_GROUP)]
    stack = lambda t: t[np.asarray(order)].reshape(
        KV_PAIRS, HEADS_PER_PAIR * BLOCK, 2 * BLOCK)
    return jnp.stack([stack(tab), stack(tab_first)])


def _attn_kernel(tiles_per_seq, sink_ref, q_ref, kvp_ref, kvc_ref, bias_ref, o_ref):
    first = (pl.program_id(0) % tiles_per_seq) == 0
    lo = lax.broadcasted_iota(jnp.int32, (2 * BLOCK, LANES), 1) < HEAD_DIM
    lo_q = lax.broadcasted_iota(jnp.int32, (BLOCK, LANES), 1) < HEAD_DIM
    for bi in range(ATTN_TQ // BLOCK):
        rows = slice(bi * BLOCK, (bi + 1) * BLOCK)
        if bi == 0:
            kvblk = jnp.concatenate([kvp_ref[...], kvc_ref[0:BLOCK, :]], axis=0)
            tab = jnp.where(first, 1, 0)
        else:
            kvblk = kvc_ref[(bi - 1) * BLOCK:(bi + 1) * BLOCK, :]
            tab = 0
        for o in range(KV_PAIRS):
            kb = kvblk[:, o * LANES:(o + 1) * LANES]
            vb = kvblk[:, KV_WIDTH + o * LANES:KV_WIDTH + (o + 1) * LANES]
            qs = jnp.concatenate(
                [q_ref[rows, (o * GQA_GROUP + m) * LANES:(o * GQA_GROUP + m + 1) * LANES]
                 for m in range(GQA_GROUP)], axis=0)
            zero = jnp.zeros_like(kb)
            s = jnp.concatenate(
                [lax.dot_general(qs, jnp.where(keep, kb, zero), (((1,), (1,)), ((), ())),
                                 preferred_element_type=F32)
                 for keep in (lo, jnp.logical_not(lo))], axis=0)
            s = s + bias_ref[tab, o]
            ps, rs = [], []
            for j in range(HEADS_PER_PAIR):
                sj = s[j * BLOCK:(j + 1) * BLOCK]
                sink = sink_ref[_paired_head(o, j // GQA_GROUP, j % GQA_GROUP)]
                mj = jnp.maximum(jnp.max(sj, axis=-1, keepdims=True), sink)
                pj = jnp.exp(sj - mj)
                den = jnp.sum(pj, axis=-1, keepdims=True) + jnp.exp(sink - mj)
                ps.append(pj.astype(BF16))
                rs.append(1.0 / den)
            ov = jnp.dot(jnp.concatenate(ps, axis=0), vb, preferred_element_type=F32)
            for m in range(GQA_GROUP):
                j0, j1 = m, GQA_GROUP + m
                o0 = ov[j0 * BLOCK:(j0 + 1) * BLOCK] * rs[j0]
                o1 = ov[j1 * BLOCK:(j1 + 1) * BLOCK] * rs[j1]
                o_ref[rows, (o * GQA_GROUP + m) * LANES:(o * GQA_GROUP + m + 1) * LANES] = (
                    jnp.where(lo_q, o0, o1).astype(BF16))


def _attention(q, kv, bias, sinks, seq):
    rows = q.shape[0]
    tq = ATTN_TQ
    blocks_per_tile = tq // BLOCK
    return pl.pallas_call(
        functools.partial(_attn_kernel, seq // tq),
        grid=(rows // tq,),
        in_specs=[pl.BlockSpec(memory_space=pltpu.SMEM),
                  pl.BlockSpec((tq, ATTN_WIDTH), lambda r: (r, 0)),
                  pl.BlockSpec((BLOCK, 2 * KV_WIDTH),
                               lambda r: (jnp.maximum(r * blocks_per_tile - 1, 0), 0)),
                  pl.BlockSpec((tq, 2 * KV_WIDTH), lambda r: (r, 0)),
                  pl.BlockSpec(bias.shape, lambda r: (0, 0, 0, 0),
                               pipeline_mode=pl.Buffered(1))],
        out_specs=pl.BlockSpec((tq, ATTN_WIDTH), lambda r: (r, 0)),
        out_shape=jax.ShapeDtypeStruct((rows, ATTN_WIDTH), BF16),
        compiler_params=pltpu.CompilerParams(
            dimension_semantics=("arbitrary",), vmem_limit_bytes=VMEM_LIMIT),
        name="attn",
    )(sinks, q, kv, kv, bias)


def _cmul(ar, ai, br, bi):
    return ar * br - ai * bi, ar * bi + ai * br


def _ssm_kernel(u_ref, perm_ref, permt_ref, wbre_ref, wbim_ref, wcre_ref, wcim_ref,
                are_ref, aim_ref, pre_ref, pim_ref, qre_ref, qim_ref, d_ref,
                wglu_ref, bglu_ref, o_ref,
                bre_scr, bim_scr, xb_scr, tre_scr, tim_scr):
    @pl.when(pl.program_id(1) == 0)
    def _():
        tre_scr[...] = jnp.zeros_like(tre_scr)
        tim_scr[...] = jnp.zeros_like(tim_scr)

    u = u_ref[...]
    u_hi = u.astype(BF16)
    u_lo = (u - u_hi.astype(F32)).astype(BF16)
    perm = perm_ref[...]
    up_hi = jnp.dot(perm, u_hi, preferred_element_type=F32)
    up = up_hi + jnp.dot(perm, u_lo, preferred_element_type=F32)
    ub = up_hi.astype(BF16)

    ct = 2 * LANES
    for t in range(SSM_LANES // ct):
        s = (t * ct // SSM_STATE * SSM_GROUP_CH) // LANES
        lhs = ub[:, s * LANES:(s + 1) * LANES]
        bre_scr[:, t * ct:(t + 1) * ct] = jnp.dot(
            lhs, wbre_ref[s * LANES:(s + 1) * LANES, t * ct:(t + 1) * ct],
            preferred_element_type=F32)
        bim_scr[:, t * ct:(t + 1) * ct] = jnp.dot(
            lhs, wbim_ref[s * LANES:(s + 1) * LANES, t * ct:(t + 1) * ct],
            preferred_element_type=F32)

    row = lax.broadcasted_iota(jnp.int32, (SUBLANES, LANES), 0)

    def lane_block(kb, carry):
        off = pl.multiple_of(kb * LANES, LANES)
        lanes = pl.ds(off, LANES)
        ar = jnp.broadcast_to(are_ref[:, lanes], (SUBLANES, LANES))
        ai = jnp.broadcast_to(aim_ref[:, lanes], (SUBLANES, LANES))
        hr = jnp.zeros((SUBLANES, LANES), F32)
        hi = jnp.zeros((SUBLANES, LANES), F32)
        for i in range(SSM_SEG):
            rows = pl.ds(i * SUBLANES, SUBLANES)
            mr, mi = _cmul(ar, ai, hr, hi)
            hr = mr + bre_scr[rows, lanes]
            hi = mi + bim_scr[rows, lanes]
            bre_scr[rows, lanes] = hr
            bim_scr[rows, lanes] = hi
        dr = pltpu.roll(jnp.where(row == SUBLANES - 1, tre_scr[:, lanes], hr), 1, 0)
        di = pltpu.roll(jnp.where(row == SUBLANES - 1, tim_scr[:, lanes], hi), 1, 0)
        for lvl, sh in enumerate((1, 2, 4)):
            qr = jnp.broadcast_to(qre_ref[pl.ds(lvl, 1), lanes], (SUBLANES, LANES))
            qi = jnp.broadcast_to(qim_ref[pl.ds(lvl, 1), lanes], (SUBLANES, LANES))
            sr = jnp.where(row >= sh, pltpu.roll(dr, sh, 0), 0.0)
            si = jnp.where(row >= sh, pltpu.roll(di, sh, 0), 0.0)
            mr, mi = _cmul(qr, qi, sr, si)
            dr, di = dr + mr, di + mi
        q1r = jnp.broadcast_to(qre_ref[pl.ds(0, 1), lanes], (SUBLANES, LANES))
        q1i = jnp.broadcast_to(qim_ref[pl.ds(0, 1), lanes], (SUBLANES, LANES))
        mr, mi = _cmul(q1r, q1i, dr, di)
        tre_scr[:, lanes] = mr + hr
        tim_scr[:, lanes] = mi + hi
        for i2 in range(SSM_SEG // 2):
            parts_r, parts_i = [], []
            for i in (2 * i2, 2 * i2 + 1):
                rows = pl.ds(i * SUBLANES, SUBLANES)
                pr = jnp.broadcast_to(pre_ref[pl.ds(i, 1), lanes], (SUBLANES, LANES))
                pi = jnp.broadcast_to(pim_ref[pl.ds(i, 1), lanes], (SUBLANES, LANES))
                mr, mi = _cmul(pr, pi, dr, di)
                parts_r.append(bre_scr[rows, lanes] + mr)
                parts_i.append(bim_scr[rows, lanes] + mi)
            rows2 = pl.ds(i2 * 2 * SUBLANES, 2 * SUBLANES)
            xb_scr[rows2, lanes] = jnp.concatenate(parts_r, axis=0).astype(BF16)
            xb_scr[rows2, pl.ds(pl.multiple_of(SSM_LANES + off, LANES), LANES)] = (
                jnp.concatenate(parts_i, axis=0).astype(BF16))
        return carry

    lax.fori_loop(0, SSM_LANES // LANES, lane_block, 0)

    kc = SSM_LANES // (SSM_WIDTH // LANES)
    ys = []
    for m in range(SSM_WIDTH // LANES):
        cols = slice(m * LANES, (m + 1) * LANES)
        yre = jnp.dot(xb_scr[:, m * kc:(m + 1) * kc], wcre_ref[m * kc:(m + 1) * kc, cols],
                      preferred_element_type=F32)
        yim = jnp.dot(xb_scr[:, SSM_LANES + m * kc:SSM_LANES + (m + 1) * kc],
                      wcim_ref[m * kc:(m + 1) * kc, cols], preferred_element_type=F32)
        ys.append(yre + yim)
    y = jnp.concatenate(ys, axis=1) + d_ref[...] * up

    z = jax.nn.gelu(y)
    gl = jnp.dot(z.astype(BF16), wglu_ref[...], preferred_element_type=F32) + bglu_ref[...]
    zg = (z * jax.nn.sigmoid(gl)).astype(BF16)
    o_ref[...] = jnp.dot(permt_ref[...], zg, preferred_element_type=F32).astype(BF16)


def _ssm(u, prep, ssm_d, w_glu, b_glu, bsz, seq):
    wbre, wbim, wcre, wcim, are, aim, pre, pim, qre, qim = prep
    perm = _time_permutation()
    chunks = seq // SSM_CHUNK
    const = lambda a: pl.BlockSpec(a.shape, lambda b, c: (0,) * a.ndim)
    consts = [jnp.asarray(perm, BF16), jnp.asarray(perm.T, BF16), wbre, wbim, wcre,
              wcim, are, aim, pre, pim, qre, qim, ssm_d.reshape(1, SSM_WIDTH),
              w_glu, b_glu.reshape(1, SSM_WIDTH)]
    return pl.pallas_call(
        _ssm_kernel,
        grid=(bsz, chunks),
        in_specs=[pl.BlockSpec((SSM_CHUNK, SSM_WIDTH), lambda b, c: (b * chunks + c, 0))]
        + [const(a) for a in consts],
        out_specs=pl.BlockSpec((SSM_CHUNK, SSM_WIDTH), lambda b, c: (b * chunks + c, 0)),
        out_shape=jax.ShapeDtypeStruct((bsz * seq, SSM_WIDTH), BF16),
        scratch_shapes=[pltpu.VMEM((SSM_CHUNK, SSM_LANES), F32),
                        pltpu.VMEM((SSM_CHUNK, SSM_LANES), F32),
                        pltpu.VMEM((SSM_CHUNK, 2 * SSM_LANES), BF16),
                        pltpu.VMEM((SUBLANES, SSM_LANES), F32),
                        pltpu.VMEM((SUBLANES, SSM_LANES), F32)],
        compiler_params=pltpu.CompilerParams(
            dimension_semantics=("arbitrary", "arbitrary"),
            vmem_limit_bytes=VMEM_LIMIT),
        name="ssm",
    )(u, *consts)


def _merge_kernel(attn_ref, zg_ref, ga_ref, gs_ref, x_ref, g1_ref,
                  wap_ref, wsp_ref, wout_ref, o_ref):
    ya = jnp.dot(attn_ref[...], wap_ref[...], preferred_element_type=F32)
    ys = jnp.dot(zg_ref[...], wsp_ref[...], preferred_element_type=F32)
    merged = ga_ref[...].astype(F32) * ya + gs_ref[...].astype(F32) * ys
    o = jnp.dot(merged.astype(BF16), wout_ref[...], preferred_element_type=F32)
    o_ref[...] = x_ref[...] + g1_ref[...] * o


def _merge(attn, zg, ga, gs, x2, g1, wap, wsp, wout, seq, tm):
    rows = x2.shape[0]
    tiles_per_seq = seq // tm
    row = lambda w: pl.BlockSpec((tm, w), lambda i: (i, 0))
    const = lambda a: pl.BlockSpec(a.shape, lambda i: (0, 0),
                                   pipeline_mode=pl.Buffered(1))
    return pl.pallas_call(
        _merge_kernel,
        grid=(rows // tm,),
        in_specs=[row(ATTN_WIDTH), row(SSM_WIDTH), row(D_MODEL), row(D_MODEL),
                  row(D_MODEL),
                  pl.BlockSpec((None, 1, D_MODEL), lambda i: (i // tiles_per_seq, 0, 0)),
                  const(wap), const(wsp), const(wout)],
        out_specs=row(D_MODEL),
        out_shape=jax.ShapeDtypeStruct((rows, D_MODEL), F32),
        compiler_params=pltpu.CompilerParams(
            dimension_semantics=("arbitrary",), vmem_limit_bytes=VMEM_LIMIT),
        name="merge",
    )(attn, zg, ga, gs, x2, g1, wap, wsp, wout)


def _ffn_kernel(x_ref, g_ref, sc_ref, sh_ref, g2_ref, fg_ref, w1_ref, w2_ref,
                o_ref, h_scr, acc_scr):
    j = pl.program_id(1)

    @pl.when(j == 0)
    def _():
        x = x_ref[...]
        ms = jnp.mean(x * x, axis=-1, keepdims=True)
        y = x * lax.rsqrt(ms + EPS)
        h = (y * g_ref[...]) * (1.0 + sc_ref[...]) + sh_ref[...]
        h_scr[...] = h.astype(BF16)
        acc_scr[...] = jnp.zeros_like(acc_scr)

    a = jnp.dot(h_scr[...], w1_ref[...], preferred_element_type=F32)
    a = jnp.square(jnp.maximum(a, 0.0)).astype(BF16)
    acc_scr[...] += jnp.dot(a, w2_ref[...], preferred_element_type=F32)

    @pl.when(j == pl.num_programs(1) - 1)
    def _():
        xo = x_ref[...] + g2_ref[...] * acc_scr[...]
        ms = jnp.mean(xo * xo, axis=-1, keepdims=True)
        o_ref[...] = (xo * lax.rsqrt(ms + EPS)) * fg_ref[...]


def _ffn(x1, g, sc, sh, g2, fg, w1, w2, seq, tm, tf):
    rows = x1.shape[0]
    tiles_per_seq = seq // tm
    bvec = lambda: pl.BlockSpec((None, 1, D_MODEL),
                                lambda i, j: (i // tiles_per_seq, 0, 0))
    vec = lambda: pl.BlockSpec((1, D_MODEL), lambda i, j: (0, 0))
    return pl.pallas_call(
        _ffn_kernel,
        grid=(rows // tm, D_FF // tf),
        in_specs=[pl.BlockSpec((tm, D_MODEL), lambda i, j: (i, 0)),
                  vec(), bvec(), bvec(), bvec(), vec(),
                  pl.BlockSpec((D_MODEL, tf), lambda i, j: (0, j)),
                  pl.BlockSpec((tf, D_MODEL), lambda i, j: (j, 0))],
        out_specs=pl.BlockSpec((tm, D_MODEL), lambda i, j: (i, 0)),
        out_shape=jax.ShapeDtypeStruct((rows, D_MODEL), F32),
        scratch_shapes=[pltpu.VMEM((tm, D_MODEL), BF16),
                        pltpu.VMEM((tm, D_MODEL), F32)],
        compiler_params=pltpu.CompilerParams(
            dimension_semantics=("arbitrary", "arbitrary"),
            vmem_limit_bytes=VMEM_LIMIT),
        name="ffn",
    )(x1, g, sc, sh, g2, fg, w1, w2)


def kernel(x, c, w_ada, b_ada, norm1_g, w_in, b_in, attn_sinks, rel_bias, lambda_re,
           lambda_im, log_step, ssm_b_re, ssm_b_im, ssm_c_re, ssm_c_im, ssm_d, w_glu,
           b_glu, w_attn_proj, w_ssm_proj, w_out, norm2_g, w_ff1, w_ff2, final_g):
    bsz, seq, _ = x.shape
    depth = w_ada.shape[0]
    assert depth == 1, "the fused final norm assumes a single layer"
    rows = bsz * seq

    bias = _bias_tables(rel_bias)

    x2 = x.reshape(rows, D_MODEL)
    for l in range(depth):
        mod = _adaln(c, w_ada[l], b_ada[l])
        sh1, sc1, g1, sh2, sc2, g2 = [
            m.reshape(bsz, 1, D_MODEL) for m in jnp.split(mod, N_MOD, axis=-1)]
        prep = _ssm_prep(lambda_re[l], lambda_im[l], log_step[l], ssm_b_re[l],
                         ssm_b_im[l], ssm_c_re[l], ssm_c_im[l])

        w_in_l = jnp.concatenate(
            [_to_paired_cols(w_in[l][:, :ATTN_WIDTH]), w_in[l][:, ATTN_WIDTH:]], axis=1)
        b_in_l = jnp.concatenate(
            [_to_paired_cols(b_in[l][:ATTN_WIDTH]), b_in[l][ATTN_WIDTH:]])
        w_ap_l = _to_paired_cols(w_attn_proj[l].T).T
        q, kv, u, ga, gs = _inproj(
            x2, norm1_g[l].reshape(1, D_MODEL), sc1, sh1, w_in_l.astype(BF16),
            b_in_l.reshape(1, IN_WIDTH), seq)
        attn = _attention(q, kv, bias, attn_sinks[l], seq)
        zg = _ssm(u, prep, ssm_d[l], w_glu[l].astype(BF16), b_glu[l], bsz, seq)
        x2 = _merge(attn, zg, ga, gs, x2, g1, w_ap_l.astype(BF16),
                    w_ssm_proj[l].astype(BF16), w_out[l].astype(BF16), seq, tm=512)
        x2 = _ffn(x2, norm2_g[l].reshape(1, D_MODEL), sc2, sh2, g2,
                  final_g.reshape(1, D_MODEL), w_ff1[l].astype(BF16), w_ff2[l].astype(BF16),
                  seq, tm=512, tf=1024)
    return x2.reshape(bsz, seq, D_MODEL)
```

```python
import functools
import math

import jax
import jax.numpy as jnp
import numpy as np
from jax import lax
from jax.experimental import pallas as pl
from jax.experimental.pallas import tpu as pltpu

F32 = jnp.float32
BF16 = jnp.bfloat16

D_MODEL = 2048
HEAD_DIM = 64
N_Q_HEADS = 16
N_KV_HEADS = 4
GQA_GROUP = N_Q_HEADS // N_KV_HEADS
ATTN_WIDTH = N_Q_HEADS * HEAD_DIM
KV_WIDTH = N_KV_HEADS * HEAD_DIM
WINDOW = 128
BLOCK = 128
NUM_BUCKETS = 32
MAX_DISTANCE = 128
NEG_INF = -1e30
SSM_WIDTH = D_MODEL // 4
SSM_GROUP_CH = 16
SSM_GROUPS = SSM_WIDTH // SSM_GROUP_CH
SSM_STATE = 64
SSM_LANES = SSM_GROUPS * SSM_STATE
D_FF = 4 * D_MODEL
IN_WIDTH = ATTN_WIDTH + 2 * KV_WIDTH + SSM_WIDTH + 2 * D_MODEL
N_MOD = 6
EPS = 1e-6

LANES = 128
SUBLANES = 8
VMEM_LIMIT = 56 * 1024 * 1024

SSM_SEG = 32
SSM_CHUNK = SUBLANES * SSM_SEG


def _time_permutation():
    r = np.arange(SSM_CHUNK)
    tok = (r % SUBLANES) * SSM_SEG + r // SUBLANES
    p = np.zeros((SSM_CHUNK, SSM_CHUNK), np.float32)
    p[r, tok] = 1.0
    return p


def _adaln_kernel(c_ref, w_ref, b_ref, o_ref):
    c = c_ref[...]
    cs = (c * jax.nn.sigmoid(c)).astype(BF16)
    o_ref[...] = jnp.dot(cs, w_ref[...].astype(BF16),
                         preferred_element_type=F32) + b_ref[...]


def _adaln(c, w_ada, b_ada):
    bsz = c.shape[0]
    n = w_ada.shape[1]
    tn = 1024
    return pl.pallas_call(
        _adaln_kernel,
        grid=(n // tn,),
        in_specs=[pl.BlockSpec((bsz, D_MODEL), lambda j: (0, 0)),
                  pl.BlockSpec((D_MODEL, tn), lambda j: (0, j)),
                  pl.BlockSpec((1, tn), lambda j: (0, j))],
        out_specs=pl.BlockSpec((bsz, tn), lambda j: (0, j)),
        out_shape=jax.ShapeDtypeStruct((bsz, n), F32),
        compiler_params=pltpu.CompilerParams(
            dimension_semantics=("arbitrary",), vmem_limit_bytes=VMEM_LIMIT),
        name="adaln",
    )(c, w_ada, b_ada.reshape(1, n))


def _ssm_prep_kernel(lre_ref, lim_ref, ls_ref, btre_ref, btim_ref, ctre_ref,
                     ctim_ref, wbre_ref, wbim_ref, wcre_ref, wcim_ref,
                     are_ref, aim_ref, pre_ref, pim_ref, qre_ref, qim_ref):
    lam_re = jnp.minimum(lre_ref[...], -1e-4)
    lam_im = lim_ref[...]
    delta = jnp.exp(ls_ref[...])
    e_re = lam_re * delta
    e_im = lam_im * delta
    mag = jnp.exp(e_re)
    abar_re = mag * jnp.cos(e_im)
    abar_im = mag * jnp.sin(e_im)
    are_ref[...] = abar_re
    aim_ref[...] = abar_im
    num_re, num_im = abar_re - 1.0, abar_im
    den = lam_re * lam_re + lam_im * lam_im
    f_re = (num_re * lam_re + num_im * lam_im) / den
    f_im = (num_im * lam_re - num_re * lam_im) / den

    rg = lax.broadcasted_iota(jnp.int32, (SSM_WIDTH, SSM_LANES), 0) // SSM_GROUP_CH
    cg = lax.broadcasted_iota(jnp.int32, (SSM_WIDTH, SSM_LANES), 1) // SSM_STATE
    diag = rg == cg
    bt_re, bt_im = btre_ref[...], btim_ref[...]
    wbre_ref[...] = jnp.where(diag, f_re * bt_re - f_im * bt_im, 0.0).astype(BF16)
    wbim_ref[...] = jnp.where(diag, f_re * bt_im + f_im * bt_re, 0.0).astype(BF16)

    rg2 = lax.broadcasted_iota(jnp.int32, (SSM_LANES, SSM_WIDTH), 0) // SSM_STATE
    cg2 = lax.broadcasted_iota(jnp.int32, (SSM_LANES, SSM_WIDTH), 1) // SSM_GROUP_CH
    diag2 = rg2 == cg2
    wcre_ref[...] = jnp.where(diag2, ctre_ref[...], 0.0).astype(BF16)
    wcim_ref[...] = jnp.where(diag2, -ctim_ref[...], 0.0).astype(BF16)

    k = (lax.broadcasted_iota(jnp.int32, (SSM_SEG, SSM_LANES), 0) + 1).astype(F32)
    pmag = jnp.exp(k * e_re)
    pre_ref[...] = pmag * jnp.cos(k * e_im)
    pim_ref[...] = pmag * jnp.sin(k * e_im)
    row = lax.broadcasted_iota(jnp.int32, (SUBLANES, SSM_LANES), 0)
    kq = jnp.where(row == 0, 1.0, jnp.where(row == 1, 2.0, 4.0)) * float(SSM_SEG)
    qmag = jnp.exp(kq * e_re)
    qre_ref[...] = qmag * jnp.cos(kq * e_im)
    qim_ref[...] = qmag * jnp.sin(kq * e_im)


def _ssm_prep(lambda_re, lambda_im, log_step, b_re, b_im, c_re, c_im):
    g, n, p = SSM_GROUPS, SSM_STATE, SSM_GROUP_CH
    lre = lambda_re.reshape(1, g * n)
    lim = lambda_im.reshape(1, g * n)
    ls = jnp.broadcast_to(log_step[:, None], (g, n)).reshape(1, g * n)

    def bt(b):
        t = jnp.transpose(b, (0, 2, 1)).reshape(g * p, n)
        return jnp.tile(t, (1, g))

    def ct(c):
        t = jnp.transpose(c, (0, 2, 1)).reshape(g * n, p)
        return jnp.tile(t, (1, g))

    full = lambda shape: pl.BlockSpec(shape, lambda: (0,) * len(shape))
    outs = [((SSM_WIDTH, SSM_LANES), BF16), ((SSM_WIDTH, SSM_LANES), BF16),
            ((SSM_LANES, SSM_WIDTH), BF16), ((SSM_LANES, SSM_WIDTH), BF16),
            ((1, SSM_LANES), F32), ((1, SSM_LANES), F32),
            ((SSM_SEG, SSM_LANES), F32), ((SSM_SEG, SSM_LANES), F32),
            ((SUBLANES, SSM_LANES), F32), ((SUBLANES, SSM_LANES), F32)]
    ins = [lre, lim, ls, bt(b_re), bt(b_im), ct(c_re), ct(c_im)]
    return pl.pallas_call(
        _ssm_prep_kernel,
        in_specs=[full(a.shape) for a in ins],
        out_specs=[full(s) for s, _ in outs],
        out_shape=[jax.ShapeDtypeStruct(s, d) for s, d in outs],
        compiler_params=pltpu.CompilerParams(vmem_limit_bytes=VMEM_LIMIT),
        name="ssm_prep",
    )(*ins)


INPROJ_TM = 512
INPROJ_SUB = 256
INPROJ_CHUNK = 512
Q_END = ATTN_WIDTH
KV_END = Q_END + 2 * KV_WIDTH
U_END = KV_END + SSM_WIDTH
GA_END = U_END + D_MODEL


def _norm_modulate(x, g, sc, sh):
    ms = jnp.mean(x * x, axis=-1, keepdims=True)
    y = x * lax.rsqrt(ms + EPS)
    return ((y * g) * (1.0 + sc) + sh).astype(BF16)


def _inproj_kernel(x_ref, g_ref, sc_ref, sh_ref, w_ref, b_ref,
                   q_ref, kv_ref, u_ref, ga_ref, gs_ref, h_scr):
    g, sc, sh = g_ref[...], sc_ref[...], sh_ref[...]
    for r0 in range(0, INPROJ_TM, INPROJ_SUB):
        rows = slice(r0, r0 + INPROJ_SUB)
        h_scr[rows, :] = _norm_modulate(x_ref[rows, :], g, sc, sh)
    for r0 in range(0, INPROJ_TM, INPROJ_SUB):
        rows = slice(r0, r0 + INPROJ_SUB)
        h = h_scr[rows, :]
        for c0 in range(0, IN_WIDTH, INPROJ_CHUNK):
            cols = slice(c0, c0 + INPROJ_CHUNK)
            acc = jnp.dot(h, w_ref[:, cols], preferred_element_type=F32) + b_ref[:, cols]
            if c0 < Q_END:
                q_ref[rows, cols] = (acc * (HEAD_DIM ** -0.5)).astype(BF16)
            elif c0 < KV_END:
                kv_ref[rows, c0 - Q_END:c0 - Q_END + INPROJ_CHUNK] = acc.astype(BF16)
            elif c0 < U_END:
                u_ref[rows, c0 - KV_END:c0 - KV_END + INPROJ_CHUNK] = acc
            elif c0 < GA_END:
                ga_ref[rows, c0 - U_END:c0 - U_END + INPROJ_CHUNK] = (
                    jax.nn.sigmoid(acc).astype(BF16))
            else:
                gs_ref[rows, c0 - GA_END:c0 - GA_END + INPROJ_CHUNK] = (
                    jax.nn.sigmoid(acc).astype(BF16))


def _inproj(x2, g, sc, sh, w, b, seq):
    rows = x2.shape[0]
    tm = INPROJ_TM
    tiles_per_seq = seq // tm
    bvec = lambda: pl.BlockSpec((None, 1, D_MODEL), lambda i: (i // tiles_per_seq, 0, 0))
    const = lambda a: pl.BlockSpec(a.shape, lambda i: (0, 0), pipeline_mode=pl.Buffered(1))
    row = lambda width: pl.BlockSpec((tm, width), lambda i: (i, 0))
    return pl.pallas_call(
        _inproj_kernel,
        grid=(rows // tm,),
        in_specs=[row(D_MODEL), const(g), bvec(), bvec(), const(w), const(b)],
        out_specs=[row(ATTN_WIDTH), row(2 * KV_WIDTH), row(SSM_WIDTH),
                   row(D_MODEL), row(D_MODEL)],
        out_shape=[jax.ShapeDtypeStruct((rows, ATTN_WIDTH), BF16),
                   jax.ShapeDtypeStruct((rows, 2 * KV_WIDTH), BF16),
                   jax.ShapeDtypeStruct((rows, SSM_WIDTH), F32),
                   jax.ShapeDtypeStruct((rows, D_MODEL), BF16),
                   jax.ShapeDtypeStruct((rows, D_MODEL), BF16)],
        scratch_shapes=[pltpu.VMEM((tm, D_MODEL), BF16)],
        compiler_params=pltpu.CompilerParams(
            dimension_semantics=("arbitrary",), vmem_limit_bytes=VMEM_LIMIT),
        name="inproj",
    )(x2, g, sc, sh, w, b)


ATTN_TQ = 512
KV_PAIRS = N_KV_HEADS // 2
HEADS_PER_PAIR = 2 * GQA_GROUP


def _paired_head(o, half, m):
    return (2 * o + half) * GQA_GROUP + m


def _to_paired_cols(w):
    lead = w.shape[:-1]
    w = w.reshape(*lead, KV_PAIRS, 2, GQA_GROUP, HEAD_DIM)
    return jnp.swapaxes(w, -3, -2).reshape(*lead, ATTN_WIDTH)


def _t5_bucket_of_distance(n):
    max_exact = NUM_BUCKETS // 2
    large = max_exact + (np.log(np.maximum(n, 1) / max_exact)
                         / np.log(MAX_DISTANCE / max_exact)
                         * (NUM_BUCKETS - max_exact)).astype(np.int32)
    large = np.minimum(large, NUM_BUCKETS - 1)
    return np.where(n < max_exact, n, large).astype(np.int32)


def _bias_tables(rel_bias):
    val = rel_bias.astype(F32)[_t5_bucket_of_distance(np.arange(WINDOW))].T
    pad = jnp.full((N_Q_HEADS, BLOCK), NEG_INF, F32)
    period = 3 * BLOCK
    f = jnp.concatenate([pad, val[:, ::-1], pad], axis=1)
    g = jnp.roll(f, -(BLOCK - 1), axis=1)
    flat = jnp.tile(g, (1, BLOCK))[:, :BLOCK * (period - 1)]
    tab = flat.reshape(N_Q_HEADS, BLOCK, period - 1)[:, :, :2 * BLOCK]
    ki = np.arange(2 * BLOCK)[None, None, :]
    tab_first = jnp.where(ki >= BLOCK, tab, NEG_INF)
    order = [_paired_head(o, half, m) for o in range(KV_PAIRS)
             for half in range(2) for m in range(GQA_GROUP)]
    stack = lambda t: t[np.asarray(order)].reshape(
        KV_PAIRS, HEADS_PER_PAIR * BLOCK, 2 * BLOCK)
    return jnp.stack([stack(tab), stack(tab_first)])


def _attn_kernel(tiles_per_seq, sink_ref, q_ref, kvp_ref, kvc_ref, bias_ref, o_ref):
    first = (pl.program_id(0) % tiles_per_seq) == 0
    lo = lax.broadcasted_iota(jnp.int32, (2 * BLOCK, LANES), 1) < HEAD_DIM
    lo_q = lax.broadcasted_iota(jnp.int32, (BLOCK, LANES), 1) < HEAD_DIM
    for bi in range(ATTN_TQ // BLOCK):
        rows = slice(bi * BLOCK, (bi + 1) * BLOCK)
        if bi == 0:
            kvblk = jnp.concatenate([kvp_ref[...], kvc_ref[0:BLOCK, :]], axis=0)
            tab = jnp.where(first, 1, 0)
        else:
            kvblk = kvc_ref[(bi - 1) * BLOCK:(bi + 1) * BLOCK, :]
            tab = 0
        for o in range(KV_PAIRS):
            kb = kvblk[:, o * LANES:(o + 1) * LANES]
            vb = kvblk[:, KV_WIDTH + o * LANES:KV_WIDTH + (o + 1) * LANES]
            qs = jnp.concatenate(
                [q_ref[rows, (o * GQA_GROUP + m) * LANES:(o * GQA_GROUP + m + 1) * LANES]
                 for m in range(GQA_GROUP)], axis=0)
            zero = jnp.zeros_like(kb)
            s = jnp.concatenate(
                [lax.dot_general(qs, jnp.where(keep, kb, zero), (((1,), (1,)), ((), ())),
                                 preferred_element_type=F32)
                 for keep in (lo, jnp.logical_not(lo))], axis=0)
            s = s + bias_ref[tab, o]
            ps, rs = [], []
            for j in range(HEADS_PER_PAIR):
                sj = s[j * BLOCK:(j + 1) * BLOCK]
                sink = sink_ref[_paired_head(o, j // GQA_GROUP, j % GQA_GROUP)]
                mj = jnp.maximum(jnp.max(sj, axis=-1, keepdims=True), sink)
                pj = jnp.exp(sj - mj)
                den = jnp.sum(pj, axis=-1, keepdims=True) + jnp.exp(sink - mj)
                ps.append(pj.astype(BF16))
                rs.append(1.0 / den)
            ov = jnp.dot(jnp.concatenate(ps, axis=0), vb, preferred_element_type=F32)
            for m in range(GQA_GROUP):
                j0, j1 = m, GQA_GROUP + m
                o0 = ov[j0 * BLOCK:(j0 + 1) * BLOCK] * rs[j0]
                o1 = ov[j1 * BLOCK:(j1 + 1) * BLOCK] * rs[j1]
                o_ref[rows, (o * GQA_GROUP + m) * LANES:(o * GQA_GROUP + m + 1) * LANES] = (
                    jnp.where(lo_q, o0, o1).astype(BF16))


def _attention(q, kv, bias, sinks, seq):
    rows = q.shape[0]
    tq = ATTN_TQ
    blocks_per_tile = tq // BLOCK
    return pl.pallas_call(
        functools.partial(_attn_kernel, seq // tq),
        grid=(rows // tq,),
        in_specs=[pl.BlockSpec(memory_space=pltpu.SMEM),
                  pl.BlockSpec((tq, ATTN_WIDTH), lambda r: (r, 0)),
                  pl.BlockSpec((BLOCK, 2 * KV_WIDTH),
                               lambda r: (jnp.maximum(r * blocks_per_tile - 1, 0), 0)),
                  pl.BlockSpec((tq, 2 * KV_WIDTH), lambda r: (r, 0)),
                  pl.BlockSpec(bias.shape, lambda r: (0, 0, 0, 0),
                               pipeline_mode=pl.Buffered(1))],
        out_specs=pl.BlockSpec((tq, ATTN_WIDTH), lambda r: (r, 0)),
        out_shape=jax.ShapeDtypeStruct((rows, ATTN_WIDTH), BF16),
        compiler_params=pltpu.CompilerParams(
            dimension_semantics=("arbitrary",), vmem_limit_bytes=VMEM_LIMIT),
        name="attn",
    )(sinks, q, kv, kv, bias)


def _cmul(ar, ai, br, bi):
    return ar * br - ai * bi, ar * bi + ai * br


def _ssm_kernel(chunks_per_seq, u_ref, perm_ref, permt_ref, wbre_ref, wbim_ref,
                wcre_ref, wcim_ref, are_ref, aim_ref, pre_ref, pim_ref, qre_ref, qim_ref,
                d_ref, wglu_ref, bglu_ref, o_ref,
                bre_scr, bim_scr, xsr_scr, xsi_scr, xb_scr, u1_scr, u2_scr,
                tre_scr, tim_scr):
    step = pl.program_id(0)

    @pl.when(step == 0)
    def _():
        for ref in (bre_scr, bim_scr, xb_scr, u1_scr, u2_scr, tre_scr, tim_scr):
            ref[...] = jnp.zeros_like(ref)

    ct = 2 * LANES
    kc = SSM_LANES // (SSM_WIDTH // ct)
    ys = []
    for m in range(SSM_WIDTH // ct):
        cols = slice(m * ct, (m + 1) * ct)
        yre = jnp.dot(xb_scr[:, m * kc:(m + 1) * kc], wcre_ref[m * kc:(m + 1) * kc, cols],
                      preferred_element_type=F32)
        yim = jnp.dot(xb_scr[:, SSM_LANES + m * kc:SSM_LANES + (m + 1) * kc],
                      wcim_ref[m * kc:(m + 1) * kc, cols], preferred_element_type=F32)
        ys.append(yre + yim)
    y = jnp.concatenate(ys, axis=1) + d_ref[...] * u2_scr[...]
    z = jax.nn.gelu(y)
    gl = jnp.dot(z.astype(BF16), wglu_ref[...], preferred_element_type=F32) + bglu_ref[...]
    zg = (z * jax.nn.sigmoid(gl)).astype(BF16)
    o_ref[...] = jnp.dot(permt_ref[...], zg, preferred_element_type=F32).astype(BF16)
    u2_scr[...] = u1_scr[...]

    keep = jnp.where((step - 1) % chunks_per_seq == 0, 0.0, 1.0)
    row = lax.broadcasted_iota(jnp.int32, (SUBLANES, LANES), 0)
    bcast = lambda ref, r, lanes: jnp.broadcast_to(ref[r:r + 1, lanes], (SUBLANES, LANES))
    for kb in range(SSM_LANES // LANES):
        lanes = slice(kb * LANES, (kb + 1) * LANES)
        lanes_im = slice(SSM_LANES + kb * LANES, SSM_LANES + (kb + 1) * LANES)
        ar, ai = bcast(are_ref, 0, lanes), bcast(aim_ref, 0, lanes)
        hr = jnp.zeros((SUBLANES, LANES), F32)
        hi = jnp.zeros((SUBLANES, LANES), F32)
        for i in range(SSM_SEG):
            rows = slice(i * SUBLANES, (i + 1) * SUBLANES)
            mr, mi = _cmul(ar, ai, hr, hi)
            hr = mr + bre_scr[rows, lanes]
            hi = mi + bim_scr[rows, lanes]
            xsr_scr[rows, lanes] = hr
            xsi_scr[rows, lanes] = hi
        dr = pltpu.roll(jnp.where(row == SUBLANES - 1, tre_scr[:, lanes] * keep, hr), 1, 0)
        di = pltpu.roll(jnp.where(row == SUBLANES - 1, tim_scr[:, lanes] * keep, hi), 1, 0)
        for lvl, sh in enumerate((1, 2, 4)):
            qr, qi = bcast(qre_ref, lvl, lanes), bcast(qim_ref, lvl, lanes)
            sr = jnp.where(row >= sh, pltpu.roll(dr, sh, 0), 0.0)
            si = jnp.where(row >= sh, pltpu.roll(di, sh, 0), 0.0)
            mr, mi = _cmul(qr, qi, sr, si)
            dr, di = dr + mr, di + mi
        mr, mi = _cmul(bcast(qre_ref, 0, lanes), bcast(qim_ref, 0, lanes), dr, di)
        tre_scr[:, lanes] = mr + hr
        tim_scr[:, lanes] = mi + hi
        for i2 in range(SSM_SEG // 2):
            parts_r, parts_i = [], []
            for i in (2 * i2, 2 * i2 + 1):
                rows = slice(i * SUBLANES, (i + 1) * SUBLANES)
                mr, mi = _cmul(bcast(pre_ref, i, lanes), bcast(pim_ref, i, lanes), dr, di)
                parts_r.append(xsr_scr[rows, lanes] + mr)
                parts_i.append(xsi_scr[rows, lanes] + mi)
            rows2 = slice(i2 * 2 * SUBLANES, (i2 + 1) * 2 * SUBLANES)
            xb_scr[rows2, lanes] = jnp.concatenate(parts_r, axis=0).astype(BF16)
            xb_scr[rows2, lanes_im] = jnp.concatenate(parts_i, axis=0).astype(BF16)

    u = u_ref[...]
    u_hi = u.astype(BF16)
    u_lo = (u - u_hi.astype(F32)).astype(BF16)
    perm = perm_ref[...]
    up_hi = jnp.dot(perm, u_hi, preferred_element_type=F32)
    u1_scr[...] = up_hi + jnp.dot(perm, u_lo, preferred_element_type=F32)
    ub = up_hi.astype(BF16)
    for t in range(SSM_LANES // ct):
        sl = (t * ct // SSM_STATE * SSM_GROUP_CH) // LANES
        lhs = ub[:, sl * LANES:(sl + 1) * LANES]
        bre_scr[:, t * ct:(t + 1) * ct] = jnp.dot(
            lhs, wbre_ref[sl * LANES:(sl + 1) * LANES, t * ct:(t + 1) * ct],
            preferred_element_type=F32)
        bim_scr[:, t * ct:(t + 1) * ct] = jnp.dot(
            lhs, wbim_ref[sl * LANES:(sl + 1) * LANES, t * ct:(t + 1) * ct],
            preferred_element_type=F32)


SSM_STAGES = 3


def _ssm(u, prep, ssm_d, w_glu, b_glu, bsz, seq):
    wbre, wbim, wcre, wcim, are, aim, pre, pim, qre, qim = prep
    perm = _time_permutation()
    chunks_per_seq = seq // SSM_CHUNK
    chunks = bsz * chunks_per_seq
    const = lambda a: pl.BlockSpec(a.shape, lambda s: (0,) * a.ndim)
    consts = [jnp.asarray(perm, BF16), jnp.asarray(perm.T, BF16), wbre, wbim, wcre,
              wcim, are, aim, pre, pim, qre, qim, ssm_d.reshape(1, SSM_WIDTH),
              w_glu, b_glu.reshape(1, SSM_WIDTH)]
    state = lambda dt, w: pltpu.VMEM((SSM_CHUNK, w), dt)
    return pl.pallas_call(
        functools.partial(_ssm_kernel, chunks_per_seq),
        grid=(chunks + SSM_STAGES - 1,),
        in_specs=[pl.BlockSpec((SSM_CHUNK, SSM_WIDTH),
                               lambda s: (jnp.minimum(s, chunks - 1), 0))]
        + [const(a) for a in consts],
        out_specs=pl.BlockSpec((SSM_CHUNK, SSM_WIDTH),
                               lambda s: (jnp.maximum(s - (SSM_STAGES - 1), 0), 0)),
        out_shape=jax.ShapeDtypeStruct((bsz * seq, SSM_WIDTH), BF16),
        scratch_shapes=[state(F32, SSM_LANES), state(F32, SSM_LANES),
                        state(F32, SSM_LANES), state(F32, SSM_LANES),
                        state(BF16, 2 * SSM_LANES),
                        state(F32, SSM_WIDTH), state(F32, SSM_WIDTH),
                        pltpu.VMEM((SUBLANES, SSM_LANES), F32),
                        pltpu.VMEM((SUBLANES, SSM_LANES), F32)],
        compiler_params=pltpu.CompilerParams(
            dimension_semantics=("arbitrary",), vmem_limit_bytes=VMEM_LIMIT),
        name="ssm",
    )(u, *consts)


def _merge_kernel(attn_ref, zg_ref, ga_ref, gs_ref, x_ref, g1_ref,
                  wap_ref, wsp_ref, wout_ref, o_ref):
    ya = jnp.dot(attn_ref[...], wap_ref[...], preferred_element_type=F32)
    ys = jnp.dot(zg_ref[...], wsp_ref[...], preferred_element_type=F32)
    merged = ga_ref[...].astype(F32) * ya + gs_ref[...].astype(F32) * ys
    o = jnp.dot(merged.astype(BF16), wout_ref[...], preferred_element_type=F32)
    o_ref[...] = x_ref[...] + g1_ref[...] * o


def _merge(attn, zg, ga, gs, x2, g1, wap, wsp, wout, seq, tm):
    rows = x2.shape[0]
    tiles_per_seq = seq // tm
    row = lambda w: pl.BlockSpec((tm, w), lambda i: (i, 0))
    const = lambda a: pl.BlockSpec(a.shape, lambda i: (0, 0),
                                   pipeline_mode=pl.Buffered(1))
    return pl.pallas_call(
        _merge_kernel,
        grid=(rows // tm,),
        in_specs=[row(ATTN_WIDTH), row(SSM_WIDTH), row(D_MODEL), row(D_MODEL),
                  row(D_MODEL),
                  pl.BlockSpec((None, 1, D_MODEL), lambda i: (i // tiles_per_seq, 0, 0)),
                  const(wap), const(wsp), const(wout)],
        out_specs=row(D_MODEL),
        out_shape=jax.ShapeDtypeStruct((rows, D_MODEL), F32),
        compiler_params=pltpu.CompilerParams(
            dimension_semantics=("arbitrary",), vmem_limit_bytes=VMEM_LIMIT),
        name="merge",
    )(attn, zg, ga, gs, x2, g1, wap, wsp, wout)


def _ffn_kernel(x_ref, g_ref, sc_ref, sh_ref, g2_ref, fg_ref, w1_ref, w2_ref,
                o_ref, h_scr, acc_scr):
    j = pl.program_id(1)

    @pl.when(j == 0)
    def _():
        x = x_ref[...]
        ms = jnp.mean(x * x, axis=-1, keepdims=True)
        y = x * lax.rsqrt(ms + EPS)
        h = (y * g_ref[...]) * (1.0 + sc_ref[...]) + sh_ref[...]
        h_scr[...] = h.astype(BF16)
        acc_scr[...] = jnp.zeros_like(acc_scr)

    a = jnp.dot(h_scr[...], w1_ref[...], preferred_element_type=F32)
    a = jnp.square(jnp.maximum(a, 0.0)).astype(BF16)
    acc_scr[...] += jnp.dot(a, w2_ref[...], preferred_element_type=F32)

    @pl.when(j == pl.num_programs(1) - 1)
    def _():
        xo = x_ref[...] + g2_ref[...] * acc_scr[...]
        ms = jnp.mean(xo * xo, axis=-1, keepdims=True)
        o_ref[...] = (xo * lax.rsqrt(ms + EPS)) * fg_ref[...]


def _ffn(x1, g, sc, sh, g2, fg, w1, w2, seq, tm, tf):
    rows = x1.shape[0]
    tiles_per_seq = seq // tm
    bvec = lambda: pl.BlockSpec((None, 1, D_MODEL),
                                lambda i, j: (i // tiles_per_seq, 0, 0))
    vec = lambda: pl.BlockSpec((1, D_MODEL), lambda i, j: (0, 0))
    return pl.pallas_call(
        _ffn_kernel,
        grid=(rows // tm, D_FF // tf),
        in_specs=[pl.BlockSpec((tm, D_MODEL), lambda i, j: (i, 0)),
                  vec(), bvec(), bvec(), bvec(), vec(),
                  pl.BlockSpec((D_MODEL, tf), lambda i, j: (0, j)),
                  pl.BlockSpec((tf, D_MODEL), lambda i, j: (j, 0))],
        out_specs=pl.BlockSpec((tm, D_MODEL), lambda i, j: (i, 0)),
        out_shape=jax.ShapeDtypeStruct((rows, D_MODEL), F32),
        scratch_shapes=[pltpu.VMEM((tm, D_MODEL), BF16),
                        pltpu.VMEM((tm, D_MODEL), F32)],
        compiler_params=pltpu.CompilerParams(
            dimension_semantics=("arbitrary", "arbitrary"),
            vmem_limit_bytes=VMEM_LIMIT),
        name="ffn",
    )(x1, g, sc, sh, g2, fg, w1, w2)


def kernel(x, c, w_ada, b_ada, norm1_g, w_in, b_in, attn_sinks, rel_bias, lambda_re,
           lambda_im, log_step, ssm_b_re, ssm_b_im, ssm_c_re, ssm_c_im, ssm_d, w_glu,
           b_glu, w_attn_proj, w_ssm_proj, w_out, norm2_g, w_ff1, w_ff2, final_g):
    bsz, seq, _ = x.shape
    depth = w_ada.shape[0]
    assert depth == 1, "the fused final norm assumes a single layer"
    rows = bsz * seq

    bias = _bias_tables(rel_bias)

    x2 = x.reshape(rows, D_MODEL)
    for l in range(depth):
        mod = _adaln(c, w_ada[l], b_ada[l])
        sh1, sc1, g1, sh2, sc2, g2 = [
            m.reshape(bsz, 1, D_MODEL) for m in jnp.split(mod, N_MOD, axis=-1)]
        prep = _ssm_prep(lambda_re[l], lambda_im[l], log_step[l], ssm_b_re[l],
                         ssm_b_im[l], ssm_c_re[l], ssm_c_im[l])

        w_in_l = jnp.concatenate(
            [_to_paired_cols(w_in[l][:, :ATTN_WIDTH]), w_in[l][:, ATTN_WIDTH:]], axis=1)
        b_in_l = jnp.concatenate(
            [_to_paired_cols(b_in[l][:ATTN_WIDTH]), b_in[l][ATTN_WIDTH:]])
        w_ap_l = _to_paired_cols(w_attn_proj[l].T).T
        q, kv, u, ga, gs = _inproj(
            x2, norm1_g[l].reshape(1, D_MODEL), sc1, sh1, w_in_l.astype(BF16),
            b_in_l.reshape(1, IN_WIDTH), seq)
        attn = _attention(q, kv, bias, attn_sinks[l], seq)
        zg = _ssm(u, prep, ssm_d[l], w_glu[l].astype(BF16), b_glu[l], bsz, seq)
        x2 = _merge(attn, zg, ga, gs, x2, g1, w_ap_l.astype(BF16),
                    w_ssm_proj[l].astype(BF16), w_out[l].astype(BF16), seq, tm=512)
        x2 = _ffn(x2, norm2_g[l].reshape(1, D_MODEL), sc2, sh2, g2,
                  final_g.reshape(1, D_MODEL), w_ff1[l].astype(BF16), w_ff2[l].astype(BF16),
                  seq, tm=512, tf=1024)
    return x2.reshape(bsz, seq, D_MODEL)
```

```python
import functools
import math

import jax
import jax.numpy as jnp
import numpy as np
from jax import lax
from jax.experimental import pallas as pl
from jax.experimental.pallas import tpu as pltpu

F32 = jnp.float32
BF16 = jnp.bfloat16

D_MODEL = 2048
HEAD_DIM = 64
N_Q_HEADS = 16
N_KV_HEADS = 4
GQA_GROUP = N_Q_HEADS // N_KV_HEADS
ATTN_WIDTH = N_Q_HEADS * HEAD_DIM
KV_WIDTH = N_KV_HEADS * HEAD_DIM
WINDOW = 128
BLOCK = 128
NUM_BUCKETS = 32
MAX_DISTANCE = 128
NEG_INF = -1e30
SSM_WIDTH = D_MODEL // 4
SSM_GROUP_CH = 16
SSM_GROUPS = SSM_WIDTH // SSM_GROUP_CH
SSM_STATE = 64
SSM_LANES = SSM_GROUPS * SSM_STATE
D_FF = 4 * D_MODEL
IN_WIDTH = ATTN_WIDTH + 2 * KV_WIDTH + SSM_WIDTH + 2 * D_MODEL
N_MOD = 6
EPS = 1e-6

LANES = 128
SUBLANES = 8
VMEM_LIMIT = 56 * 1024 * 1024

SSM_SEG = 32
SSM_CHUNK = SUBLANES * SSM_SEG


def _time_permutation():
    r = np.arange(SSM_CHUNK)
    tok = (r % SUBLANES) * SSM_SEG + r // SUBLANES
    p = np.zeros((SSM_CHUNK, SSM_CHUNK), np.float32)
    p[r, tok] = 1.0
    return p


def _adaln_kernel(c_ref, w_ref, b_ref, o_ref):
    c = c_ref[...]
    cs = (c * jax.nn.sigmoid(c)).astype(BF16)
    o_ref[...] = jnp.dot(cs, w_ref[...].astype(BF16),
                         preferred_element_type=F32) + b_ref[...]


def _adaln(c, w_ada, b_ada):
    bsz = c.shape[0]
    n = w_ada.shape[1]
    tn = 1024
    return pl.pallas_call(
        _adaln_kernel,
        grid=(n // tn,),
        in_specs=[pl.BlockSpec((bsz, D_MODEL), lambda j: (0, 0)),
                  pl.BlockSpec((D_MODEL, tn), lambda j: (0, j)),
                  pl.BlockSpec((1, tn), lambda j: (0, j))],
        out_specs=pl.BlockSpec((bsz, tn), lambda j: (0, j)),
        out_shape=jax.ShapeDtypeStruct((bsz, n), F32),
        compiler_params=pltpu.CompilerParams(
            dimension_semantics=("arbitrary",), vmem_limit_bytes=VMEM_LIMIT),
        name="adaln",
    )(c, w_ada, b_ada.reshape(1, n))


def _ssm_prep_kernel(lre_ref, lim_ref, ls_ref, btre_ref, btim_ref, ctre_ref,
                     ctim_ref, wbre_ref, wbim_ref, wcre_ref, wcim_ref,
                     are_ref, aim_ref, pre_ref, pim_ref, qre_ref, qim_ref):
    lam_re = jnp.minimum(lre_ref[...], -1e-4)
    lam_im = lim_ref[...]
    delta = jnp.exp(ls_ref[...])
    e_re = lam_re * delta
    e_im = lam_im * delta
    mag = jnp.exp(e_re)
    abar_re = mag * jnp.cos(e_im)
    abar_im = mag * jnp.sin(e_im)
    are_ref[...] = abar_re
    aim_ref[...] = abar_im
    num_re, num_im = abar_re - 1.0, abar_im
    den = lam_re * lam_re + lam_im * lam_im
    f_re = (num_re * lam_re + num_im * lam_im) / den
    f_im = (num_im * lam_re - num_re * lam_im) / den

    rg = lax.broadcasted_iota(jnp.int32, (SSM_WIDTH, SSM_LANES), 0) // SSM_GROUP_CH
    cg = lax.broadcasted_iota(jnp.int32, (SSM_WIDTH, SSM_LANES), 1) // SSM_STATE
    diag = rg == cg
    bt_re, bt_im = btre_ref[...], btim_ref[...]
    wbre_ref[...] = jnp.where(diag, f_re * bt_re - f_im * bt_im, 0.0).astype(BF16)
    wbim_ref[...] = jnp.where(diag, f_re * bt_im + f_im * bt_re, 0.0).astype(BF16)

    rg2 = lax.broadcasted_iota(jnp.int32, (SSM_LANES, SSM_WIDTH), 0) // SSM_STATE
    cg2 = lax.broadcasted_iota(jnp.int32, (SSM_LANES, SSM_WIDTH), 1) // SSM_GROUP_CH
    diag2 = rg2 == cg2
    wcre_ref[...] = jnp.where(diag2, ctre_ref[...], 0.0).astype(BF16)
    wcim_ref[...] = jnp.where(diag2, -ctim_ref[...], 0.0).astype(BF16)

    k = (lax.broadcasted_iota(jnp.int32, (SSM_SEG, SSM_LANES), 0) + 1).astype(F32)
    pmag = jnp.exp(k * e_re)
    pre_ref[...] = pmag * jnp.cos(k * e_im)
    pim_ref[...] = pmag * jnp.sin(k * e_im)
    row = lax.broadcasted_iota(jnp.int32, (SUBLANES, SSM_LANES), 0)
    kq = jnp.where(row == 0, 1.0, jnp.where(row == 1, 2.0, 4.0)) * float(SSM_SEG)
    qmag = jnp.exp(kq * e_re)
    qre_ref[...] = qmag * jnp.cos(kq * e_im)
    qim_ref[...] = qmag * jnp.sin(kq * e_im)


def _ssm_prep(lambda_re, lambda_im, log_step, b_re, b_im, c_re, c_im):
    g, n, p = SSM_GROUPS, SSM_STATE, SSM_GROUP_CH
    lre = lambda_re.reshape(1, g * n)
    lim = lambda_im.reshape(1, g * n)
    ls = jnp.broadcast_to(log_step[:, None], (g, n)).reshape(1, g * n)

    def bt(b):
        t = jnp.transpose(b, (0, 2, 1)).reshape(g * p, n)
        return jnp.tile(t, (1, g))

    def ct(c):
        t = jnp.transpose(c, (0, 2, 1)).reshape(g * n, p)
        return jnp.tile(t, (1, g))

    full = lambda shape: pl.BlockSpec(shape, lambda: (0,) * len(shape))
    outs = [((SSM_WIDTH, SSM_LANES), BF16), ((SSM_WIDTH, SSM_LANES), BF16),
            ((SSM_LANES, SSM_WIDTH), BF16), ((SSM_LANES, SSM_WIDTH), BF16),
            ((1, SSM_LANES), F32), ((1, SSM_LANES), F32),
            ((SSM_SEG, SSM_LANES), F32), ((SSM_SEG, SSM_LANES), F32),
            ((SUBLANES, SSM_LANES), F32), ((SUBLANES, SSM_LANES), F32)]
    ins = [lre, lim, ls, bt(b_re), bt(b_im), ct(c_re), ct(c_im)]
    return pl.pallas_call(
        _ssm_prep_kernel,
        in_specs=[full(a.shape) for a in ins],
        out_specs=[full(s) for s, _ in outs],
        out_shape=[jax.ShapeDtypeStruct(s, d) for s, d in outs],
        compiler_params=pltpu.CompilerParams(vmem_limit_bytes=VMEM_LIMIT),
        name="ssm_prep",
    )(*ins)


INPROJ_TM = 256
INPROJ_SUB = 256
INPROJ_CHUNK = 512
Q_END = ATTN_WIDTH
KV_END = Q_END + 2 * KV_WIDTH
U_END = KV_END + SSM_WIDTH
GA_END = U_END + D_MODEL


def _norm_modulate(x, g, sc, sh):
    ms = jnp.mean(x * x, axis=-1, keepdims=True)
    y = x * lax.rsqrt(ms + EPS)
    return ((y * g) * (1.0 + sc) + sh).astype(BF16)


def _inproj_kernel(n_cast, x_ref, g_ref, sc_ref, sh_ref, w_ref, b_ref, *refs):
    cast_in, refs = refs[:n_cast], refs[n_cast:]
    q_ref, kv_ref, u_ref, ga_ref, gs_ref = refs[:5]
    cast_out, h_scr = refs[5:5 + n_cast], refs[5 + n_cast]
    for src, dst in zip(cast_in, cast_out):
        dst[...] = src[...].astype(BF16)
    g, sc, sh = g_ref[...], sc_ref[...], sh_ref[...]
    for r0 in range(0, INPROJ_TM, INPROJ_SUB):
        rows = slice(r0, r0 + INPROJ_SUB)
        h_scr[rows, :] = _norm_modulate(x_ref[rows, :], g, sc, sh)
    for r0 in range(0, INPROJ_TM, INPROJ_SUB):
        rows = slice(r0, r0 + INPROJ_SUB)
        h = h_scr[rows, :]
        for c0 in range(0, IN_WIDTH, INPROJ_CHUNK):
            cols = slice(c0, c0 + INPROJ_CHUNK)
            acc = jnp.dot(h, w_ref[:, cols], preferred_element_type=F32) + b_ref[:, cols]
            if c0 < Q_END:
                q_ref[rows, cols] = (acc * (HEAD_DIM ** -0.5)).astype(BF16)
            elif c0 < KV_END:
                kv_ref[rows, c0 - Q_END:c0 - Q_END + INPROJ_CHUNK] = acc.astype(BF16)
            elif c0 < U_END:
                u_ref[rows, c0 - KV_END:c0 - KV_END + INPROJ_CHUNK] = acc
            elif c0 < GA_END:
                ga_ref[rows, c0 - U_END:c0 - U_END + INPROJ_CHUNK] = (
                    jax.nn.sigmoid(acc).astype(BF16))
            else:
                gs_ref[rows, c0 - GA_END:c0 - GA_END + INPROJ_CHUNK] = (
                    jax.nn.sigmoid(acc).astype(BF16))


def _inproj(x2, g, sc, sh, w, b, seq, cast_weights):
    rows = x2.shape[0]
    tm = INPROJ_TM
    tiles_per_seq = seq // tm
    steps = rows // tm
    cast_spec = lambda a: pl.BlockSpec((a.shape[0] // steps, a.shape[1]), lambda i: (i, 0))
    bvec = lambda: pl.BlockSpec((None, 1, D_MODEL), lambda i: (i // tiles_per_seq, 0, 0))
    const = lambda a: pl.BlockSpec(a.shape, lambda i: (0, 0), pipeline_mode=pl.Buffered(1))
    row = lambda width: pl.BlockSpec((tm, width), lambda i: (i, 0))
    return pl.pallas_call(
        functools.partial(_inproj_kernel, len(cast_weights)),
        grid=(steps,),
        in_specs=[row(D_MODEL), const(g), bvec(), bvec(), const(w), const(b)]
        + [cast_spec(a) for a in cast_weights],
        out_specs=[row(ATTN_WIDTH), row(2 * KV_WIDTH), row(SSM_WIDTH),
                   row(D_MODEL), row(D_MODEL)] + [cast_spec(a) for a in cast_weights],
        out_shape=[jax.ShapeDtypeStruct((rows, ATTN_WIDTH), BF16),
                   jax.ShapeDtypeStruct((rows, 2 * KV_WIDTH), BF16),
                   jax.ShapeDtypeStruct((rows, SSM_WIDTH), F32),
                   jax.ShapeDtypeStruct((rows, D_MODEL), BF16),
                   jax.ShapeDtypeStruct((rows, D_MODEL), BF16)]
        + [jax.ShapeDtypeStruct(a.shape, BF16) for a in cast_weights],
        scratch_shapes=[pltpu.VMEM((tm, D_MODEL), BF16)],
        compiler_params=pltpu.CompilerParams(
            dimension_semantics=("arbitrary",), vmem_limit_bytes=VMEM_LIMIT),
        name="inproj",
    )(x2, g, sc, sh, w, b, *cast_weights)


ATTN_TQ = 512
KV_PAIRS = N_KV_HEADS // 2
HEADS_PER_PAIR = 2 * GQA_GROUP


def _paired_head(o, half, m):
    return (2 * o + half) * GQA_GROUP + m


def _to_paired_cols(w):
    lead = w.shape[:-1]
    w = w.reshape(*lead, KV_PAIRS, 2, GQA_GROUP, HEAD_DIM)
    return jnp.swapaxes(w, -3, -2).reshape(*lead, ATTN_WIDTH)


def _t5_bucket_of_distance(n):
    max_exact = NUM_BUCKETS // 2
    large = max_exact + (np.log(np.maximum(n, 1) / max_exact)
                         / np.log(MAX_DISTANCE / max_exact)
                         * (NUM_BUCKETS - max_exact)).astype(np.int32)
    large = np.minimum(large, NUM_BUCKETS - 1)
    return np.where(n < max_exact, n, large).astype(np.int32)


def _bias_rows(rel_bias):
    val = rel_bias.astype(F32)[_t5_bucket_of_distance(np.arange(WINDOW))].T
    neg = lambda n: jnp.full((N_Q_HEADS, n), NEG_INF, F32)
    return jnp.concatenate([neg(1), val[:, ::-1], neg(BLOCK - 1)], axis=1)


def _attn_kernel(tiles_per_seq, sink_ref, q_ref, kvp_ref, kvc_ref, w_ref, o_ref, bias_ref):
    @pl.when(pl.program_id(0) == 0)
    def _():
        ki = lax.broadcasted_iota(jnp.int32, (BLOCK, 2 * BLOCK), 1)
        for o in range(KV_PAIRS):
            for j in range(HEADS_PER_PAIR):
                a = _paired_head(o, j // GQA_GROUP, j % GQA_GROUP)
                w = jnp.broadcast_to(w_ref[a:a + 1, :], (BLOCK, 2 * BLOCK))
                tab = pltpu.roll(w, 0, 1, stride=1, stride_axis=0)
                bias_ref[0, o, j * BLOCK:(j + 1) * BLOCK, :] = tab
                bias_ref[1, o, j * BLOCK:(j + 1) * BLOCK, :] = jnp.where(
                    ki >= BLOCK, tab, NEG_INF)

    first = (pl.program_id(0) % tiles_per_seq) == 0
    lo = lax.broadcasted_iota(jnp.int32, (2 * BLOCK, LANES), 1) < HEAD_DIM
    lo_q = lax.broadcasted_iota(jnp.int32, (BLOCK, LANES), 1) < HEAD_DIM
    for bi in range(ATTN_TQ // BLOCK):
        rows = slice(bi * BLOCK, (bi + 1) * BLOCK)
        if bi == 0:
            kvblk = jnp.concatenate([kvp_ref[...], kvc_ref[0:BLOCK, :]], axis=0)
            tab = jnp.where(first, 1, 0)
        else:
            kvblk = kvc_ref[(bi - 1) * BLOCK:(bi + 1) * BLOCK, :]
            tab = 0
        for o in range(KV_PAIRS):
            kb = kvblk[:, o * LANES:(o + 1) * LANES]
            vb = kvblk[:, KV_WIDTH + o * LANES:KV_WIDTH + (o + 1) * LANES]
            qs = jnp.concatenate(
                [q_ref[rows, (o * GQA_GROUP + m) * LANES:(o * GQA_GROUP + m + 1) * LANES]
                 for m in range(GQA_GROUP)], axis=0)
            zero = jnp.zeros_like(kb)
            s = jnp.concatenate(
                [lax.dot_general(qs, jnp.where(keep, kb, zero), (((1,), (1,)), ((), ())),
                                 preferred_element_type=F32)
                 for keep in (lo, jnp.logical_not(lo))], axis=0)
            s = s + bias_ref[tab, o]
            ps, rs = [], []
            for j in range(HEADS_PER_PAIR):
                sj = s[j * BLOCK:(j + 1) * BLOCK]
                sink = sink_ref[_paired_head(o, j // GQA_GROUP, j % GQA_GROUP)]
                mj = jnp.maximum(jnp.max(sj, axis=-1, keepdims=True), sink)
                pj = jnp.exp(sj - mj)
                den = jnp.sum(pj, axis=-1, keepdims=True) + jnp.exp(sink - mj)
                ps.append(pj.astype(BF16))
                rs.append(1.0 / den)
            ov = jnp.dot(jnp.concatenate(ps, axis=0), vb, preferred_element_type=F32)
            for m in range(GQA_GROUP):
                j0, j1 = m, GQA_GROUP + m
                o0 = ov[j0 * BLOCK:(j0 + 1) * BLOCK] * rs[j0]
                o1 = ov[j1 * BLOCK:(j1 + 1) * BLOCK] * rs[j1]
                o_ref[rows, (o * GQA_GROUP + m) * LANES:(o * GQA_GROUP + m + 1) * LANES] = (
                    jnp.where(lo_q, o0, o1).astype(BF16))


def _attention(q, kv, bias_rows, sinks, seq):
    rows = q.shape[0]
    tq = ATTN_TQ
    blocks_per_tile = tq // BLOCK
    return pl.pallas_call(
        functools.partial(_attn_kernel, seq // tq),
        grid=(rows // tq,),
        in_specs=[pl.BlockSpec(memory_space=pltpu.SMEM),
                  pl.BlockSpec((tq, ATTN_WIDTH), lambda r: (r, 0)),
                  pl.BlockSpec((BLOCK, 2 * KV_WIDTH),
                               lambda r: (jnp.maximum(r * blocks_per_tile - 1, 0), 0)),
                  pl.BlockSpec((tq, 2 * KV_WIDTH), lambda r: (r, 0)),
                  pl.BlockSpec(bias_rows.shape, lambda r: (0, 0))],
        out_specs=pl.BlockSpec((tq, ATTN_WIDTH), lambda r: (r, 0)),
        out_shape=jax.ShapeDtypeStruct((rows, ATTN_WIDTH), BF16),
        scratch_shapes=[pltpu.VMEM((2, KV_PAIRS, HEADS_PER_PAIR * BLOCK, 2 * BLOCK), F32)],
        compiler_params=pltpu.CompilerParams(
            dimension_semantics=("arbitrary",), vmem_limit_bytes=VMEM_LIMIT),
        name="attn",
    )(sinks, q, kv, kv, bias_rows)


def _cmul(ar, ai, br, bi):
    return ar * br - ai * bi, ar * bi + ai * br


def _ssm_kernel(chunks_per_seq, u_ref, perm_ref, permt_ref, wbre_ref, wbim_ref,
                wcre_ref, wcim_ref, are_ref, aim_ref, pre_ref, pim_ref, qre_ref, qim_ref,
                d_ref, wglu_ref, bglu_ref, o_ref,
                bre_scr, bim_scr, xsr_scr, xsi_scr, xb_scr, u1_scr, u2_scr,
                tre_scr, tim_scr):
    step = pl.program_id(0)

    @pl.when(step == 0)
    def _():
        for ref in (bre_scr, bim_scr, xb_scr, u1_scr, u2_scr, tre_scr, tim_scr):
            ref[...] = jnp.zeros_like(ref)

    ct = 2 * LANES
    kc = SSM_LANES // (SSM_WIDTH // ct)
    ys = []
    for m in range(SSM_WIDTH // ct):
        cols = slice(m * ct, (m + 1) * ct)
        yre = jnp.dot(xb_scr[:, m * kc:(m + 1) * kc], wcre_ref[m * kc:(m + 1) * kc, cols],
                      preferred_element_type=F32)
        yim = jnp.dot(xb_scr[:, SSM_LANES + m * kc:SSM_LANES + (m + 1) * kc],
                      wcim_ref[m * kc:(m + 1) * kc, cols], preferred_element_type=F32)
        ys.append(yre + yim)
    y = jnp.concatenate(ys, axis=1) + d_ref[...] * u2_scr[...]
    z = jax.nn.gelu(y)
    gl = jnp.dot(z.astype(BF16), wglu_ref[...], preferred_element_type=F32) + bglu_ref[...]
    zg = (z * jax.nn.sigmoid(gl)).astype(BF16)
    o_ref[...] = jnp.dot(permt_ref[...], zg, preferred_element_type=F32).astype(BF16)
    u2_scr[...] = u1_scr[...]

    keep = jnp.where((step - 1) % chunks_per_seq == 0, 0.0, 1.0)
    row = lax.broadcasted_iota(jnp.int32, (SUBLANES, LANES), 0)
    bcast = lambda ref, r, lanes: jnp.broadcast_to(ref[r:r + 1, lanes], (SUBLANES, LANES))
    for kb in range(SSM_LANES // LANES):
        lanes = slice(kb * LANES, (kb + 1) * LANES)
        lanes_im = slice(SSM_LANES + kb * LANES, SSM_LANES + (kb + 1) * LANES)
        ar, ai = bcast(are_ref, 0, lanes), bcast(aim_ref, 0, lanes)
        hr = jnp.zeros((SUBLANES, LANES), F32)
        hi = jnp.zeros((SUBLANES, LANES), F32)
        for i in range(SSM_SEG):
            rows = slice(i * SUBLANES, (i + 1) * SUBLANES)
            mr, mi = _cmul(ar, ai, hr, hi)
            hr = mr + bre_scr[rows, lanes]
            hi = mi + bim_scr[rows, lanes]
            xsr_scr[rows, lanes] = hr
            xsi_scr[rows, lanes] = hi
        dr = pltpu.roll(jnp.where(row == SUBLANES - 1, tre_scr[:, lanes] * keep, hr), 1, 0)
        di = pltpu.roll(jnp.where(row == SUBLANES - 1, tim_scr[:, lanes] * keep, hi), 1, 0)
        for lvl, sh in enumerate((1, 2, 4)):
            qr, qi = bcast(qre_ref, lvl, lanes), bcast(qim_ref, lvl, lanes)
            sr = jnp.where(row >= sh, pltpu.roll(dr, sh, 0), 0.0)
            si = jnp.where(row >= sh, pltpu.roll(di, sh, 0), 0.0)
            mr, mi = _cmul(qr, qi, sr, si)
            dr, di = dr + mr, di + mi
        mr, mi = _cmul(bcast(qre_ref, 0, lanes), bcast(qim_ref, 0, lanes), dr, di)
        tre_scr[:, lanes] = mr + hr
        tim_scr[:, lanes] = mi + hi
        for i2 in range(SSM_SEG // 2):
            parts_r, parts_i = [], []
            for i in (2 * i2, 2 * i2 + 1):
                rows = slice(i * SUBLANES, (i + 1) * SUBLANES)
                mr, mi = _cmul(bcast(pre_ref, i, lanes), bcast(pim_ref, i, lanes), dr, di)
                parts_r.append(xsr_scr[rows, lanes] + mr)
                parts_i.append(xsi_scr[rows, lanes] + mi)
            rows2 = slice(i2 * 2 * SUBLANES, (i2 + 1) * 2 * SUBLANES)
            xb_scr[rows2, lanes] = jnp.concatenate(parts_r, axis=0).astype(BF16)
            xb_scr[rows2, lanes_im] = jnp.concatenate(parts_i, axis=0).astype(BF16)

    u = u_ref[...]
    u_hi = u.astype(BF16)
    u_lo = (u - u_hi.astype(F32)).astype(BF16)
    perm = perm_ref[...]
    up_hi = jnp.dot(perm, u_hi, preferred_element_type=F32)
    u1_scr[...] = up_hi + jnp.dot(perm, u_lo, preferred_element_type=F32)
    ub = up_hi.astype(BF16)
    for t in range(SSM_LANES // ct):
        sl = (t * ct // SSM_STATE * SSM_GROUP_CH) // LANES
        lhs = ub[:, sl * LANES:(sl + 1) * LANES]
        bre_scr[:, t * ct:(t + 1) * ct] = jnp.dot(
            lhs, wbre_ref[sl * LANES:(sl + 1) * LANES, t * ct:(t + 1) * ct],
            preferred_element_type=F32)
        bim_scr[:, t * ct:(t + 1) * ct] = jnp.dot(
            lhs, wbim_ref[sl * LANES:(sl + 1) * LANES, t * ct:(t + 1) * ct],
            preferred_element_type=F32)


SSM_STAGES = 3


def _ssm(u, prep, ssm_d, w_glu, b_glu, bsz, seq):
    wbre, wbim, wcre, wcim, are, aim, pre, pim, qre, qim = prep
    perm = _time_permutation()
    chunks_per_seq = seq // SSM_CHUNK
    chunks = bsz * chunks_per_seq
    const = lambda a: pl.BlockSpec(a.shape, lambda s: (0,) * a.ndim)
    consts = [jnp.asarray(perm, BF16), jnp.asarray(perm.T, BF16), wbre, wbim, wcre,
              wcim, are, aim, pre, pim, qre, qim, ssm_d.reshape(1, SSM_WIDTH),
              w_glu, b_glu.reshape(1, SSM_WIDTH)]
    state = lambda dt, w: pltpu.VMEM((SSM_CHUNK, w), dt)
    return pl.pallas_call(
        functools.partial(_ssm_kernel, chunks_per_seq),
        grid=(chunks + SSM_STAGES - 1,),
        in_specs=[pl.BlockSpec((SSM_CHUNK, SSM_WIDTH),
                               lambda s: (jnp.minimum(s, chunks - 1), 0))]
        + [const(a) for a in consts],
        out_specs=pl.BlockSpec((SSM_CHUNK, SSM_WIDTH),
                               lambda s: (jnp.maximum(s - (SSM_STAGES - 1), 0), 0)),
        out_shape=jax.ShapeDtypeStruct((bsz * seq, SSM_WIDTH), BF16),
        scratch_shapes=[state(F32, SSM_LANES), state(F32, SSM_LANES),
                        state(F32, SSM_LANES), state(F32, SSM_LANES),
                        state(BF16, 2 * SSM_LANES),
                        state(F32, SSM_WIDTH), state(F32, SSM_WIDTH),
                        pltpu.VMEM((SUBLANES, SSM_LANES), F32),
                        pltpu.VMEM((SUBLANES, SSM_LANES), F32)],
        compiler_params=pltpu.CompilerParams(
            dimension_semantics=("arbitrary",), vmem_limit_bytes=VMEM_LIMIT),
        name="ssm",
    )(u, *consts)


def _merge_kernel(attn_ref, zg_ref, ga_ref, gs_ref, x_ref, g1_ref,
                  wap_ref, wsp_ref, wout_ref, o_ref):
    ya = jnp.dot(attn_ref[...], wap_ref[...], preferred_element_type=F32)
    ys = jnp.dot(zg_ref[...], wsp_ref[...], preferred_element_type=F32)
    merged = ga_ref[...].astype(F32) * ya + gs_ref[...].astype(F32) * ys
    o = jnp.dot(merged.astype(BF16), wout_ref[...], preferred_element_type=F32)
    o_ref[...] = x_ref[...] + g1_ref[...] * o


def _merge(attn, zg, ga, gs, x2, g1, wap, wsp, wout, seq, tm):
    rows = x2.shape[0]
    tiles_per_seq = seq // tm
    row = lambda w: pl.BlockSpec((tm, w), lambda i: (i, 0))
    const = lambda a: pl.BlockSpec(a.shape, lambda i: (0, 0),
                                   pipeline_mode=pl.Buffered(1))
    return pl.pallas_call(
        _merge_kernel,
        grid=(rows // tm,),
        in_specs=[row(ATTN_WIDTH), row(SSM_WIDTH), row(D_MODEL), row(D_MODEL),
                  row(D_MODEL),
                  pl.BlockSpec((None, 1, D_MODEL), lambda i: (i // tiles_per_seq, 0, 0)),
                  const(wap), const(wsp), const(wout)],
        out_specs=row(D_MODEL),
        out_shape=jax.ShapeDtypeStruct((rows, D_MODEL), F32),
        compiler_params=pltpu.CompilerParams(
            dimension_semantics=("arbitrary",), vmem_limit_bytes=VMEM_LIMIT),
        name="merge",
    )(attn, zg, ga, gs, x2, g1, wap, wsp, wout)


def _ffn_kernel(x_ref, g_ref, sc_ref, sh_ref, g2_ref, fg_ref, w1_ref, w2_ref,
                o_ref, h_scr, acc_scr):
    j = pl.program_id(1)

    @pl.when(j == 0)
    def _():
        x = x_ref[...]
        ms = jnp.mean(x * x, axis=-1, keepdims=True)
        y = x * lax.rsqrt(ms + EPS)
        h = (y * g_ref[...]) * (1.0 + sc_ref[...]) + sh_ref[...]
        h_scr[...] = h.astype(BF16)
        acc_scr[...] = jnp.zeros_like(acc_scr)

    a = jnp.dot(h_scr[...], w1_ref[...], preferred_element_type=F32)
    a = jnp.square(jnp.maximum(a, 0.0)).astype(BF16)
    acc_scr[...] += jnp.dot(a, w2_ref[...], preferred_element_type=F32)

    @pl.when(j == pl.num_programs(1) - 1)
    def _():
        xo = x_ref[...] + g2_ref[...] * acc_scr[...]
        ms = jnp.mean(xo * xo, axis=-1, keepdims=True)
        o_ref[...] = (xo * lax.rsqrt(ms + EPS)) * fg_ref[...]


def _ffn(x1, g, sc, sh, g2, fg, w1, w2, seq, tm, tf):
    rows = x1.shape[0]
    tiles_per_seq = seq // tm
    bvec = lambda: pl.BlockSpec((None, 1, D_MODEL),
                                lambda i, j: (i // tiles_per_seq, 0, 0))
    vec = lambda: pl.BlockSpec((1, D_MODEL), lambda i, j: (0, 0))
    return pl.pallas_call(
        _ffn_kernel,
        grid=(rows // tm, D_FF // tf),
        in_specs=[pl.BlockSpec((tm, D_MODEL), lambda i, j: (i, 0)),
                  vec(), bvec(), bvec(), bvec(), vec(),
                  pl.BlockSpec((D_MODEL, tf), lambda i, j: (0, j)),
                  pl.BlockSpec((tf, D_MODEL), lambda i, j: (j, 0))],
        out_specs=pl.BlockSpec((tm, D_MODEL), lambda i, j: (i, 0)),
        out_shape=jax.ShapeDtypeStruct((rows, D_MODEL), F32),
        scratch_shapes=[pltpu.VMEM((tm, D_MODEL), BF16),
                        pltpu.VMEM((tm, D_MODEL), F32)],
        compiler_params=pltpu.CompilerParams(
            dimension_semantics=("arbitrary", "arbitrary"),
            vmem_limit_bytes=VMEM_LIMIT),
        name="ffn",
    )(x1, g, sc, sh, g2, fg, w1, w2)


def kernel(x, c, w_ada, b_ada, norm1_g, w_in, b_in, attn_sinks, rel_bias, lambda_re,
           lambda_im, log_step, ssm_b_re, ssm_b_im, ssm_c_re, ssm_c_im, ssm_d, w_glu,
           b_glu, w_attn_proj, w_ssm_proj, w_out, norm2_g, w_ff1, w_ff2, final_g):
    bsz, seq, _ = x.shape
    depth = w_ada.shape[0]
    assert depth == 1, "the fused final norm assumes a single layer"
    rows = bsz * seq

    bias_rows = _bias_rows(rel_bias)

    x2 = x.reshape(rows, D_MODEL)
    for l in range(depth):
        mod = _adaln(c, w_ada[l], b_ada[l])
        sh1, sc1, g1, sh2, sc2, g2 = [
            m.reshape(bsz, 1, D_MODEL) for m in jnp.split(mod, N_MOD, axis=-1)]
        prep = _ssm_prep(lambda_re[l], lambda_im[l], log_step[l], ssm_b_re[l],
                         ssm_b_im[l], ssm_c_re[l], ssm_c_im[l])

        w_in_l = jnp.concatenate(
            [_to_paired_cols(w_in[l][:, :ATTN_WIDTH]), w_in[l][:, ATTN_WIDTH:]], axis=1)
        b_in_l = jnp.concatenate(
            [_to_paired_cols(b_in[l][:ATTN_WIDTH]), b_in[l][ATTN_WIDTH:]])
        w_ap_l = _to_paired_cols(w_attn_proj[l].T).T
        q, kv, u, ga, gs, w_out_b, w_ff1_b, w_ff2_b = _inproj(
            x2, norm1_g[l].reshape(1, D_MODEL), sc1, sh1, w_in_l.astype(BF16),
            b_in_l.reshape(1, IN_WIDTH), seq, (w_out[l], w_ff1[l], w_ff2[l]))
        attn = _attention(q, kv, bias_rows, attn_sinks[l], seq)
        zg = _ssm(u, prep, ssm_d[l], w_glu[l].astype(BF16), b_glu[l], bsz, seq)
        x2 = _merge(attn, zg, ga, gs, x2, g1, w_ap_l.astype(BF16),
                    w_ssm_proj[l].astype(BF16), w_out_b, seq, tm=512)
        x2 = _ffn(x2, norm2_g[l].reshape(1, D_MODEL), sc2, sh2, g2,
                  final_g.reshape(1, D_MODEL), w_ff1_b, w_ff2_b,
                  seq, tm=512, tf=1024)
    return x2.reshape(bsz, seq, D_MODEL)
```

```python
import functools
import math

import jax
import jax.numpy as jnp
import numpy as np
from jax import lax
from jax.experimental import pallas as pl
from jax.experimental.pallas import tpu as pltpu

F32 = jnp.float32
BF16 = jnp.bfloat16

D_MODEL = 2048
HEAD_DIM = 64
N_Q_HEADS = 16
N_KV_HEADS = 4
GQA_GROUP = N_Q_HEADS // N_KV_HEADS
ATTN_WIDTH = N_Q_HEADS * HEAD_DIM
KV_WIDTH = N_KV_HEADS * HEAD_DIM
WINDOW = 128
BLOCK = 128
NUM_BUCKETS = 32
MAX_DISTANCE = 128
NEG_INF = -1e30
SSM_WIDTH = D_MODEL // 4
SSM_GROUP_CH = 16
SSM_GROUPS = SSM_WIDTH // SSM_GROUP_CH
SSM_STATE = 64
SSM_LANES = SSM_GROUPS * SSM_STATE
D_FF = 4 * D_MODEL
IN_WIDTH = ATTN_WIDTH + 2 * KV_WIDTH + SSM_WIDTH + 2 * D_MODEL
N_MOD = 6
EPS = 1e-6

LANES = 128
SUBLANES = 8
V7X_VMEM_BYTES = 64 * 1024 * 1024
VMEM_LIMIT = V7X_VMEM_BYTES - 8 * 1024 * 1024
FFN_VMEM_LIMIT = V7X_VMEM_BYTES - 4 * 1024 * 1024

SSM_SEG = 32
SSM_CHUNK = SUBLANES * SSM_SEG


def _time_permutation():
    r = np.arange(SSM_CHUNK)
    tok = (r % SUBLANES) * SSM_SEG + r // SUBLANES
    p = np.zeros((SSM_CHUNK, SSM_CHUNK), np.float32)
    p[r, tok] = 1.0
    return p


def _adaln_kernel(c_ref, w_ref, b_ref, o_ref):
    c = c_ref[...]
    cs = (c * jax.nn.sigmoid(c)).astype(BF16)
    o_ref[...] = jnp.dot(cs, w_ref[...].astype(BF16),
                         preferred_element_type=F32) + b_ref[...]


def _adaln(c, w_ada, b_ada):
    bsz = c.shape[0]
    n = w_ada.shape[1]
    tn = 1024
    return pl.pallas_call(
        _adaln_kernel,
        grid=(n // tn,),
        in_specs=[pl.BlockSpec((bsz, D_MODEL), lambda j: (0, 0)),
                  pl.BlockSpec((D_MODEL, tn), lambda j: (0, j)),
                  pl.BlockSpec((1, tn), lambda j: (0, j))],
        out_specs=pl.BlockSpec((bsz, tn), lambda j: (0, j)),
        out_shape=jax.ShapeDtypeStruct((bsz, n), F32),
        compiler_params=pltpu.CompilerParams(
            dimension_semantics=("arbitrary",), vmem_limit_bytes=VMEM_LIMIT),
        name="adaln",
    )(c, w_ada, b_ada.reshape(1, n))


def _ssm_prep_kernel(lre_ref, lim_ref, ls_ref, btre_ref, btim_ref, ctre_ref,
                     ctim_ref, wbre_ref, wbim_ref, wcre_ref, wcim_ref,
                     are_ref, aim_ref, pre_ref, pim_ref, qre_ref, qim_ref):
    lam_re = jnp.minimum(lre_ref[...], -1e-4)
    lam_im = lim_ref[...]
    delta = jnp.exp(ls_ref[...])
    e_re = lam_re * delta
    e_im = lam_im * delta
    mag = jnp.exp(e_re)
    abar_re = mag * jnp.cos(e_im)
    abar_im = mag * jnp.sin(e_im)
    are_ref[...] = abar_re
    aim_ref[...] = abar_im
    num_re, num_im = abar_re - 1.0, abar_im
    den = lam_re * lam_re + lam_im * lam_im
    f_re = (num_re * lam_re + num_im * lam_im) / den
    f_im = (num_im * lam_re - num_re * lam_im) / den

    rg = lax.broadcasted_iota(jnp.int32, (SSM_WIDTH, SSM_LANES), 0) // SSM_GROUP_CH
    cg = lax.broadcasted_iota(jnp.int32, (SSM_WIDTH, SSM_LANES), 1) // SSM_STATE
    diag = rg == cg
    bt_re, bt_im = btre_ref[...], btim_ref[...]
    wbre_ref[...] = jnp.where(diag, f_re * bt_re - f_im * bt_im, 0.0).astype(BF16)
    wbim_ref[...] = jnp.where(diag, f_re * bt_im + f_im * bt_re, 0.0).astype(BF16)

    rg2 = lax.broadcasted_iota(jnp.int32, (SSM_LANES, SSM_WIDTH), 0) // SSM_STATE
    cg2 = lax.broadcasted_iota(jnp.int32, (SSM_LANES, SSM_WIDTH), 1) // SSM_GROUP_CH
    diag2 = rg2 == cg2
    wcre_ref[...] = jnp.where(diag2, ctre_ref[...], 0.0).astype(BF16)
    wcim_ref[...] = jnp.where(diag2, -ctim_ref[...], 0.0).astype(BF16)

    k = (lax.broadcasted_iota(jnp.int32, (SSM_SEG, SSM_LANES), 0) + 1).astype(F32)
    pmag = jnp.exp(k * e_re)
    pre_ref[...] = pmag * jnp.cos(k * e_im)
    pim_ref[...] = pmag * jnp.sin(k * e_im)
    row = lax.broadcasted_iota(jnp.int32, (SUBLANES, SSM_LANES), 0)
    kq = jnp.where(row == 0, 1.0, jnp.where(row == 1, 2.0, 4.0)) * float(SSM_SEG)
    qmag = jnp.exp(kq * e_re)
    qre_ref[...] = qmag * jnp.cos(kq * e_im)
    qim_ref[...] = qmag * jnp.sin(kq * e_im)


def _ssm_prep(lambda_re, lambda_im, log_step, b_re, b_im, c_re, c_im):
    g, n, p = SSM_GROUPS, SSM_STATE, SSM_GROUP_CH
    lre = lambda_re.reshape(1, g * n)
    lim = lambda_im.reshape(1, g * n)
    ls = jnp.broadcast_to(log_step[:, None], (g, n)).reshape(1, g * n)

    def bt(b):
        t = jnp.transpose(b, (0, 2, 1)).reshape(g * p, n)
        return jnp.tile(t, (1, g))

    def ct(c):
        t = jnp.transpose(c, (0, 2, 1)).reshape(g * n, p)
        return jnp.tile(t, (1, g))

    full = lambda shape: pl.BlockSpec(shape, lambda: (0,) * len(shape))
    outs = [((SSM_WIDTH, SSM_LANES), BF16), ((SSM_WIDTH, SSM_LANES), BF16),
            ((SSM_LANES, SSM_WIDTH), BF16), ((SSM_LANES, SSM_WIDTH), BF16),
            ((1, SSM_LANES), F32), ((1, SSM_LANES), F32),
            ((SSM_SEG, SSM_LANES), F32), ((SSM_SEG, SSM_LANES), F32),
            ((SUBLANES, SSM_LANES), F32), ((SUBLANES, SSM_LANES), F32)]
    ins = [lre, lim, ls, bt(b_re), bt(b_im), ct(c_re), ct(c_im)]
    return pl.pallas_call(
        _ssm_prep_kernel,
        in_specs=[full(a.shape) for a in ins],
        out_specs=[full(s) for s, _ in outs],
        out_shape=[jax.ShapeDtypeStruct(s, d) for s, d in outs],
        compiler_params=pltpu.CompilerParams(vmem_limit_bytes=VMEM_LIMIT),
        name="ssm_prep",
    )(*ins)


INPROJ_TM = 256
INPROJ_SUB = 256
INPROJ_CHUNK = 512
Q_END = ATTN_WIDTH
KV_END = Q_END + 2 * KV_WIDTH
U_END = KV_END + SSM_WIDTH
GA_END = U_END + D_MODEL


def _norm_modulate(x, g, sc, sh):
    ms = jnp.mean(x * x, axis=-1, keepdims=True)
    y = x * lax.rsqrt(ms + EPS)
    return ((y * g) * (1.0 + sc) + sh).astype(BF16)


def _inproj_kernel(n_cast, x_ref, g_ref, sc_ref, sh_ref, wq_ref, wr_ref, b_ref, *refs):
    cast_in, refs = refs[:n_cast], refs[n_cast:]
    q_ref, kv_ref, u_ref, ga_ref, gs_ref = refs[:5]
    cast_out, h_scr = refs[5:5 + n_cast], refs[5 + n_cast]
    for src, dst in zip(cast_in, cast_out):
        dst[...] = src[...].astype(BF16)
    g, sc, sh = g_ref[...], sc_ref[...], sh_ref[...]
    for r0 in range(0, INPROJ_TM, INPROJ_SUB):
        rows = slice(r0, r0 + INPROJ_SUB)
        h_scr[rows, :] = _norm_modulate(x_ref[rows, :], g, sc, sh)
    for r0 in range(0, INPROJ_TM, INPROJ_SUB):
        rows = slice(r0, r0 + INPROJ_SUB)
        h = h_scr[rows, :]
        for c0 in range(0, IN_WIDTH, INPROJ_CHUNK):
            cols = slice(c0, c0 + INPROJ_CHUNK)
            w = (wq_ref[:, cols] if c0 < Q_END
                 else wr_ref[:, c0 - Q_END:c0 - Q_END + INPROJ_CHUNK])
            acc = jnp.dot(h, w, preferred_element_type=F32) + b_ref[:, cols]
            if c0 < Q_END:
                q_ref[rows, cols] = (acc * (HEAD_DIM ** -0.5)).astype(BF16)
            elif c0 < KV_END:
                kv_ref[rows, c0 - Q_END:c0 - Q_END + INPROJ_CHUNK] = acc.astype(BF16)
            elif c0 < U_END:
                u_ref[rows, c0 - KV_END:c0 - KV_END + INPROJ_CHUNK] = acc
            elif c0 < GA_END:
                ga_ref[rows, c0 - U_END:c0 - U_END + INPROJ_CHUNK] = (
                    jax.nn.sigmoid(acc).astype(BF16))
            else:
                gs_ref[rows, c0 - GA_END:c0 - GA_END + INPROJ_CHUNK] = (
                    jax.nn.sigmoid(acc).astype(BF16))


def _inproj(x2, g, sc, sh, wq, wr, b, seq, cast_weights):
    rows = x2.shape[0]
    tm = INPROJ_TM
    tiles_per_seq = seq // tm
    steps = rows // tm
    cast_spec = lambda a: pl.BlockSpec((a.shape[0] // steps, a.shape[1]), lambda i: (i, 0))
    bvec = lambda: pl.BlockSpec((None, 1, D_MODEL), lambda i: (i // tiles_per_seq, 0, 0))
    const = lambda a: pl.BlockSpec(a.shape, lambda i: (0, 0), pipeline_mode=pl.Buffered(1))
    row = lambda width: pl.BlockSpec((tm, width), lambda i: (i, 0))
    return pl.pallas_call(
        functools.partial(_inproj_kernel, len(cast_weights)),
        grid=(steps,),
        in_specs=[row(D_MODEL), const(g), bvec(), bvec(), const(wq), const(wr), const(b)]
        + [cast_spec(a) for a in cast_weights],
        out_specs=[row(ATTN_WIDTH), row(2 * KV_WIDTH), row(SSM_WIDTH),
                   row(D_MODEL), row(D_MODEL)] + [cast_spec(a) for a in cast_weights],
        out_shape=[jax.ShapeDtypeStruct((rows, ATTN_WIDTH), BF16),
                   jax.ShapeDtypeStruct((rows, 2 * KV_WIDTH), BF16),
                   jax.ShapeDtypeStruct((rows, SSM_WIDTH), F32),
                   jax.ShapeDtypeStruct((rows, D_MODEL), BF16),
                   jax.ShapeDtypeStruct((rows, D_MODEL), BF16)]
        + [jax.ShapeDtypeStruct(a.shape, BF16) for a in cast_weights],
        scratch_shapes=[pltpu.VMEM((tm, D_MODEL), BF16)],
        compiler_params=pltpu.CompilerParams(
            dimension_semantics=("arbitrary",), vmem_limit_bytes=VMEM_LIMIT),
        name="inproj",
    )(x2, g, sc, sh, wq, wr, b, *cast_weights)


ATTN_TQ = 512
KV_PAIRS = N_KV_HEADS // 2
HEADS_PER_PAIR = 2 * GQA_GROUP


def _paired_head(o, half, m):
    return (2 * o + half) * GQA_GROUP + m


def _to_paired_cols(w):
    lead = w.shape[:-1]
    w = w.reshape(*lead, KV_PAIRS, 2, GQA_GROUP, HEAD_DIM)
    return jnp.swapaxes(w, -3, -2).reshape(*lead, ATTN_WIDTH)


def _to_paired_rows(w):
    tail = w.shape[1:]
    w = w.reshape(KV_PAIRS, 2, GQA_GROUP, HEAD_DIM, *tail)
    return jnp.swapaxes(w, 1, 2).reshape(ATTN_WIDTH, *tail)


def _t5_bucket_of_distance(n):
    max_exact = NUM_BUCKETS // 2
    large = max_exact + (np.log(np.maximum(n, 1) / max_exact)
                         / np.log(MAX_DISTANCE / max_exact)
                         * (NUM_BUCKETS - max_exact)).astype(np.int32)
    large = np.minimum(large, NUM_BUCKETS - 1)
    return np.where(n < max_exact, n, large).astype(np.int32)


def _bias_rows(rel_bias):
    val = rel_bias.astype(F32)[_t5_bucket_of_distance(np.arange(WINDOW))].T
    neg = lambda n: jnp.full((N_Q_HEADS, n), NEG_INF, F32)
    return jnp.concatenate([neg(1), val[:, ::-1], neg(BLOCK - 1)], axis=1)


def _attn_kernel(tiles_per_seq, sink_ref, q_ref, kvp_ref, kvc_ref, w_ref, o_ref, bias_ref):
    @pl.when(pl.program_id(0) == 0)
    def _():
        ki = lax.broadcasted_iota(jnp.int32, (BLOCK, 2 * BLOCK), 1)
        for o in range(KV_PAIRS):
            for j in range(HEADS_PER_PAIR):
                a = _paired_head(o, j // GQA_GROUP, j % GQA_GROUP)
                w = jnp.broadcast_to(w_ref[a:a + 1, :], (BLOCK, 2 * BLOCK))
                tab = pltpu.roll(w, 0, 1, stride=1, stride_axis=0)
                bias_ref[0, o, j * BLOCK:(j + 1) * BLOCK, :] = tab
                bias_ref[1, o, j * BLOCK:(j + 1) * BLOCK, :] = jnp.where(
                    ki >= BLOCK, tab, NEG_INF)

    first = (pl.program_id(0) % tiles_per_seq) == 0
    lo = lax.broadcasted_iota(jnp.int32, (2 * BLOCK, LANES), 1) < HEAD_DIM
    lo_q = lax.broadcasted_iota(jnp.int32, (BLOCK, LANES), 1) < HEAD_DIM
    for bi in range(ATTN_TQ // BLOCK):
        rows = slice(bi * BLOCK, (bi + 1) * BLOCK)
        if bi == 0:
            kvblk = jnp.concatenate([kvp_ref[...], kvc_ref[0:BLOCK, :]], axis=0)
            tab = jnp.where(first, 1, 0)
        else:
            kvblk = kvc_ref[(bi - 1) * BLOCK:(bi + 1) * BLOCK, :]
            tab = 0
        for o in range(KV_PAIRS):
            kb = kvblk[:, o * LANES:(o + 1) * LANES]
            vb = kvblk[:, KV_WIDTH + o * LANES:KV_WIDTH + (o + 1) * LANES]
            qs = jnp.concatenate(
                [q_ref[rows, (o * GQA_GROUP + m) * LANES:(o * GQA_GROUP + m + 1) * LANES]
                 for m in range(GQA_GROUP)], axis=0)
            zero = jnp.zeros_like(kb)
            s = jnp.concatenate(
                [lax.dot_general(qs, jnp.where(keep, kb, zero), (((1,), (1,)), ((), ())),
                                 preferred_element_type=F32)
                 for keep in (lo, jnp.logical_not(lo))], axis=0)
            s = s + bias_ref[tab, o]
            ps, rs = [], []
            for j in range(HEADS_PER_PAIR):
                sj = s[j * BLOCK:(j + 1) * BLOCK]
                sink = sink_ref[_paired_head(o, j // GQA_GROUP, j % GQA_GROUP)]
                mj = jnp.maximum(jnp.max(sj, axis=-1, keepdims=True), sink)
                pj = jnp.exp(sj - mj)
                den = jnp.sum(pj, axis=-1, keepdims=True) + jnp.exp(sink - mj)
                ps.append(pj.astype(BF16))
                rs.append(1.0 / den)
            ov = jnp.dot(jnp.concatenate(ps, axis=0), vb, preferred_element_type=F32)
            for m in range(GQA_GROUP):
                j0, j1 = m, GQA_GROUP + m
                o0 = ov[j0 * BLOCK:(j0 + 1) * BLOCK] * rs[j0]
                o1 = ov[j1 * BLOCK:(j1 + 1) * BLOCK] * rs[j1]
                o_ref[rows, (o * GQA_GROUP + m) * LANES:(o * GQA_GROUP + m + 1) * LANES] = (
                    jnp.where(lo_q, o0, o1).astype(BF16))


def _attention(q, kv, bias_rows, sinks, seq):
    rows = q.shape[0]
    tq = ATTN_TQ
    blocks_per_tile = tq // BLOCK
    return pl.pallas_call(
        functools.partial(_attn_kernel, seq // tq),
        grid=(rows // tq,),
        in_specs=[pl.BlockSpec(memory_space=pltpu.SMEM),
                  pl.BlockSpec((tq, ATTN_WIDTH), lambda r: (r, 0)),
                  pl.BlockSpec((BLOCK, 2 * KV_WIDTH),
                               lambda r: (jnp.maximum(r * blocks_per_tile - 1, 0), 0)),
                  pl.BlockSpec((tq, 2 * KV_WIDTH), lambda r: (r, 0)),
                  pl.BlockSpec(bias_rows.shape, lambda r: (0, 0))],
        out_specs=pl.BlockSpec((tq, ATTN_WIDTH), lambda r: (r, 0)),
        out_shape=jax.ShapeDtypeStruct((rows, ATTN_WIDTH), BF16),
        scratch_shapes=[pltpu.VMEM((2, KV_PAIRS, HEADS_PER_PAIR * BLOCK, 2 * BLOCK), F32)],
        compiler_params=pltpu.CompilerParams(
            dimension_semantics=("arbitrary",), vmem_limit_bytes=VMEM_LIMIT),
        name="attn",
    )(sinks, q, kv, kv, bias_rows)


def _cmul(ar, ai, br, bi):
    return ar * br - ai * bi, ar * bi + ai * br


def _ssm_kernel(chunks_per_seq, u_ref, perm_ref, permt_ref, wbre_ref, wbim_ref,
                wcre_ref, wcim_ref, are_ref, aim_ref, pre_ref, pim_ref, qre_ref, qim_ref,
                d_ref, wglu_ref, bglu_ref, o_ref,
                bre_scr, bim_scr, xsr_scr, xsi_scr, xb_scr, u1_scr, u2_scr,
                tre_scr, tim_scr):
    step = pl.program_id(0)

    @pl.when(step == 0)
    def _():
        for ref in (bre_scr, bim_scr, xb_scr, u1_scr, u2_scr, tre_scr, tim_scr):
            ref[...] = jnp.zeros_like(ref)

    ct = 2 * LANES
    kc = SSM_LANES // (SSM_WIDTH // ct)
    ys = []
    for m in range(SSM_WIDTH // ct):
        cols = slice(m * ct, (m + 1) * ct)
        yre = jnp.dot(xb_scr[:, m * kc:(m + 1) * kc], wcre_ref[m * kc:(m + 1) * kc, cols],
                      preferred_element_type=F32)
        yim = jnp.dot(xb_scr[:, SSM_LANES + m * kc:SSM_LANES + (m + 1) * kc],
                      wcim_ref[m * kc:(m + 1) * kc, cols], preferred_element_type=F32)
        ys.append(yre + yim)
    y = jnp.concatenate(ys, axis=1) + d_ref[...] * u2_scr[...]
    z = jax.nn.gelu(y)
    gl = jnp.dot(z.astype(BF16), wglu_ref[...], preferred_element_type=F32) + bglu_ref[...]
    zg = (z * jax.nn.sigmoid(gl)).astype(BF16)
    o_ref[...] = jnp.dot(permt_ref[...], zg, preferred_element_type=F32).astype(BF16)
    u2_scr[...] = u1_scr[...]

    keep = jnp.where((step - 1) % chunks_per_seq == 0, 0.0, 1.0)
    row = lax.broadcasted_iota(jnp.int32, (SUBLANES, LANES), 0)
    bcast = lambda ref, r, lanes: jnp.broadcast_to(ref[r:r + 1, lanes], (SUBLANES, LANES))
    for kb in range(SSM_LANES // LANES):
        lanes = slice(kb * LANES, (kb + 1) * LANES)
        lanes_im = slice(SSM_LANES + kb * LANES, SSM_LANES + (kb + 1) * LANES)
        ar, ai = bcast(are_ref, 0, lanes), bcast(aim_ref, 0, lanes)
        hr = jnp.zeros((SUBLANES, LANES), F32)
        hi = jnp.zeros((SUBLANES, LANES), F32)
        for i in range(SSM_SEG):
            rows = slice(i * SUBLANES, (i + 1) * SUBLANES)
            mr, mi = _cmul(ar, ai, hr, hi)
            hr = mr + bre_scr[rows, lanes]
            hi = mi + bim_scr[rows, lanes]
            xsr_scr[rows, lanes] = hr
            xsi_scr[rows, lanes] = hi
        dr = pltpu.roll(jnp.where(row == SUBLANES - 1, tre_scr[:, lanes] * keep, hr), 1, 0)
        di = pltpu.roll(jnp.where(row == SUBLANES - 1, tim_scr[:, lanes] * keep, hi), 1, 0)
        for lvl, sh in enumerate((1, 2, 4)):
            qr, qi = bcast(qre_ref, lvl, lanes), bcast(qim_ref, lvl, lanes)
            sr = jnp.where(row >= sh, pltpu.roll(dr, sh, 0), 0.0)
            si = jnp.where(row >= sh, pltpu.roll(di, sh, 0), 0.0)
            mr, mi = _cmul(qr, qi, sr, si)
            dr, di = dr + mr, di + mi
        mr, mi = _cmul(bcast(qre_ref, 0, lanes), bcast(qim_ref, 0, lanes), dr, di)
        tre_scr[:, lanes] = mr + hr
        tim_scr[:, lanes] = mi + hi
        for i2 in range(SSM_SEG // 2):
            parts_r, parts_i = [], []
            for i in (2 * i2, 2 * i2 + 1):
                rows = slice(i * SUBLANES, (i + 1) * SUBLANES)
                mr, mi = _cmul(bcast(pre_ref, i, lanes), bcast(pim_ref, i, lanes), dr, di)
                parts_r.append(xsr_scr[rows, lanes] + mr)
                parts_i.append(xsi_scr[rows, lanes] + mi)
            rows2 = slice(i2 * 2 * SUBLANES, (i2 + 1) * 2 * SUBLANES)
            xb_scr[rows2, lanes] = jnp.concatenate(parts_r, axis=0).astype(BF16)
            xb_scr[rows2, lanes_im] = jnp.concatenate(parts_i, axis=0).astype(BF16)

    u = u_ref[...]
    u_hi = u.astype(BF16)
    u_lo = (u - u_hi.astype(F32)).astype(BF16)
    perm = perm_ref[...]
    up_hi = jnp.dot(perm, u_hi, preferred_element_type=F32)
    u1_scr[...] = up_hi + jnp.dot(perm, u_lo, preferred_element_type=F32)
    ub = up_hi.astype(BF16)
    for t in range(SSM_LANES // ct):
        sl = (t * ct // SSM_STATE * SSM_GROUP_CH) // LANES
        lhs = ub[:, sl * LANES:(sl + 1) * LANES]
        bre_scr[:, t * ct:(t + 1) * ct] = jnp.dot(
            lhs, wbre_ref[sl * LANES:(sl + 1) * LANES, t * ct:(t + 1) * ct],
            preferred_element_type=F32)
        bim_scr[:, t * ct:(t + 1) * ct] = jnp.dot(
            lhs, wbim_ref[sl * LANES:(sl + 1) * LANES, t * ct:(t + 1) * ct],
            preferred_element_type=F32)


SSM_STAGES = 3


def _ssm(u, prep, ssm_d, w_glu, b_glu, bsz, seq):
    wbre, wbim, wcre, wcim, are, aim, pre, pim, qre, qim = prep
    perm = _time_permutation()
    chunks_per_seq = seq // SSM_CHUNK
    chunks = bsz * chunks_per_seq
    const = lambda a: pl.BlockSpec(a.shape, lambda s: (0,) * a.ndim)
    consts = [jnp.asarray(perm, BF16), jnp.asarray(perm.T, BF16), wbre, wbim, wcre,
              wcim, are, aim, pre, pim, qre, qim, ssm_d.reshape(1, SSM_WIDTH),
              w_glu, b_glu.reshape(1, SSM_WIDTH)]
    state = lambda dt, w: pltpu.VMEM((SSM_CHUNK, w), dt)
    return pl.pallas_call(
        functools.partial(_ssm_kernel, chunks_per_seq),
        grid=(chunks + SSM_STAGES - 1,),
        in_specs=[pl.BlockSpec((SSM_CHUNK, SSM_WIDTH),
                               lambda s: (jnp.minimum(s, chunks - 1), 0))]
        + [const(a) for a in consts],
        out_specs=pl.BlockSpec((SSM_CHUNK, SSM_WIDTH),
                               lambda s: (jnp.maximum(s - (SSM_STAGES - 1), 0), 0)),
        out_shape=jax.ShapeDtypeStruct((bsz * seq, SSM_WIDTH), BF16),
        scratch_shapes=[state(F32, SSM_LANES), state(F32, SSM_LANES),
                        state(F32, SSM_LANES), state(F32, SSM_LANES),
                        state(BF16, 2 * SSM_LANES),
                        state(F32, SSM_WIDTH), state(F32, SSM_WIDTH),
                        pltpu.VMEM((SUBLANES, SSM_LANES), F32),
                        pltpu.VMEM((SUBLANES, SSM_LANES), F32)],
        compiler_params=pltpu.CompilerParams(
            dimension_semantics=("arbitrary",), vmem_limit_bytes=VMEM_LIMIT),
        name="ssm",
    )(u, *consts)


def _merge_kernel(attn_ref, zg_ref, ga_ref, gs_ref, x_ref, g1_ref,
                  wap_ref, wsp_ref, wout_ref, o_ref):
    ya = jnp.dot(attn_ref[...], wap_ref[...], preferred_element_type=F32)
    ys = jnp.dot(zg_ref[...], wsp_ref[...], preferred_element_type=F32)
    merged = ga_ref[...].astype(F32) * ya + gs_ref[...].astype(F32) * ys
    o = jnp.dot(merged.astype(BF16), wout_ref[...], preferred_element_type=F32)
    o_ref[...] = x_ref[...] + g1_ref[...] * o


def _merge(attn, zg, ga, gs, x2, g1, wap, wsp, wout, seq, tm):
    rows = x2.shape[0]
    tiles_per_seq = seq // tm
    row = lambda w: pl.BlockSpec((tm, w), lambda i: (i, 0))
    const = lambda a: pl.BlockSpec(a.shape, lambda i: (0, 0),
                                   pipeline_mode=pl.Buffered(1))
    return pl.pallas_call(
        _merge_kernel,
        grid=(rows // tm,),
        in_specs=[row(ATTN_WIDTH), row(SSM_WIDTH), row(D_MODEL), row(D_MODEL),
                  row(D_MODEL),
                  pl.BlockSpec((None, 1, D_MODEL), lambda i: (i // tiles_per_seq, 0, 0)),
                  const(wap), const(wsp), const(wout)],
        out_specs=row(D_MODEL),
        out_shape=jax.ShapeDtypeStruct((rows, D_MODEL), F32),
        compiler_params=pltpu.CompilerParams(
            dimension_semantics=("arbitrary",), vmem_limit_bytes=VMEM_LIMIT),
        name="merge",
    )(attn, zg, ga, gs, x2, g1, wap, wsp, wout)


FFN_STAGES = 3


def _ffn_kernel(n_tiles, x_ref, g_ref, sc_ref, sh_ref, g2_ref, fg_ref, w1_ref, w2_ref,
                o_ref, h0, h1, acc0, acc1, res0, res1):
    t, j = pl.program_id(0), pl.program_id(1)
    sub = x_ref.shape[0] // pl.num_programs(1)
    rows = pl.ds(pl.multiple_of(j * sub, sub), sub)

    @pl.when((t == 0) & (j == 0))
    def _():
        for ref in (h0, h1, acc0, acc1, res0, res1):
            ref[...] = jnp.zeros_like(ref)

    def epilogue(res, acc):
        xo = res[rows, :] + g2_ref[...] * acc[rows, :]
        ms = jnp.mean(xo * xo, axis=-1, keepdims=True)
        o_ref[rows, :] = (xo * lax.rsqrt(ms + EPS)) * fg_ref[...]
        acc[rows, :] = jnp.zeros((sub, D_MODEL), F32)

    def prologue(res, h):
        x = x_ref[rows, :]
        res[rows, :] = x
        h[rows, :] = _norm_modulate(x, g_ref[...], sc_ref[...], sh_ref[...])

    def matmul(h, acc):
        a = jnp.dot(h[...], w1_ref[...], preferred_element_type=F32)
        a = jnp.square(jnp.maximum(a, 0.0)).astype(BF16)
        acc[...] += jnp.dot(a, w2_ref[...], preferred_element_type=F32)

    bufs = ((h0, acc0, res0), (h1, acc1, res1))
    for p in range(2):
        (h_p, acc_p, res_p), (h_q, acc_q, _) = bufs[p], bufs[1 - p]
        parity = t % 2 == p

        @pl.when(parity & (t == 0))
        def _():
            prologue(res_p, h_p)

        @pl.when(parity & (t >= 1) & (t <= n_tiles))
        def _():
            epilogue(res_p, acc_p)
            prologue(res_p, h_p)
            matmul(h_q, acc_q)

        @pl.when(parity & (t == n_tiles + 1))
        def _():
            epilogue(res_p, acc_p)


def _ffn(x1, g, sc, sh, g2, fg, w1, w2, seq, tm, tf):
    rows = x1.shape[0]
    n_tiles = rows // tm
    n_ff = D_FF // tf
    tiles_per_seq = seq // tm
    in_tile = lambda t: jnp.minimum(t, n_tiles - 1)
    out_tile = lambda t: jnp.clip(t - (FFN_STAGES - 1), 0, n_tiles - 1)
    ff_blk = lambda t, j: jnp.where(t == 0, 0, jnp.where(t == n_tiles + 1, n_ff - 1, j))
    bvec = lambda tile: pl.BlockSpec(
        (None, 1, D_MODEL), lambda t, j: (tile(t) // tiles_per_seq, 0, 0))
    vec = lambda: pl.BlockSpec((1, D_MODEL), lambda t, j: (0, 0))
    tile_buf = lambda dt: pltpu.VMEM((tm, D_MODEL), dt)
    return pl.pallas_call(
        functools.partial(_ffn_kernel, n_tiles),
        grid=(n_tiles + FFN_STAGES - 1, n_ff),
        in_specs=[pl.BlockSpec((tm, D_MODEL), lambda t, j: (in_tile(t), 0)),
                  vec(), bvec(in_tile), bvec(in_tile), bvec(out_tile), vec(),
                  pl.BlockSpec((D_MODEL, tf), lambda t, j: (0, ff_blk(t, j))),
                  pl.BlockSpec((tf, D_MODEL), lambda t, j: (ff_blk(t, j), 0))],
        out_specs=pl.BlockSpec((tm, D_MODEL), lambda t, j: (out_tile(t), 0)),
        out_shape=jax.ShapeDtypeStruct((rows, D_MODEL), F32),
        scratch_shapes=[tile_buf(BF16), tile_buf(BF16), tile_buf(F32), tile_buf(F32),
                        tile_buf(F32), tile_buf(F32)],
        compiler_params=pltpu.CompilerParams(
            dimension_semantics=("arbitrary", "arbitrary"),
            vmem_limit_bytes=FFN_VMEM_LIMIT),
        name="ffn",
    )(x1, g, sc, sh, g2, fg, w1, w2)


def kernel(x, c, w_ada, b_ada, norm1_g, w_in, b_in, attn_sinks, rel_bias, lambda_re,
           lambda_im, log_step, ssm_b_re, ssm_b_im, ssm_c_re, ssm_c_im, ssm_d, w_glu,
           b_glu, w_attn_proj, w_ssm_proj, w_out, norm2_g, w_ff1, w_ff2, final_g):
    bsz, seq, _ = x.shape
    depth = w_ada.shape[0]
    assert depth == 1, "the fused final norm assumes a single layer"
    rows = bsz * seq

    bias_rows = _bias_rows(rel_bias)

    x2 = x.reshape(rows, D_MODEL)
    for l in range(depth):
        mod = _adaln(c, w_ada[l], b_ada[l])
        sh1, sc1, g1, sh2, sc2, g2 = [
            m.reshape(bsz, 1, D_MODEL) for m in jnp.split(mod, N_MOD, axis=-1)]
        prep = _ssm_prep(lambda_re[l], lambda_im[l], log_step[l], ssm_b_re[l],
                         ssm_b_im[l], ssm_c_re[l], ssm_c_im[l])

        w_q = _to_paired_cols(w_in[l][:, :ATTN_WIDTH]).astype(BF16)
        w_rest = w_in[l][:, ATTN_WIDTH:].astype(BF16)
        b_in_l = jnp.concatenate(
            [_to_paired_cols(b_in[l][:ATTN_WIDTH]), b_in[l][ATTN_WIDTH:]])
        w_ap_l = _to_paired_rows(w_attn_proj[l])
        q, kv, u, ga, gs, w_out_b, w_ff1_b, w_ff2_b = _inproj(
            x2, norm1_g[l].reshape(1, D_MODEL), sc1, sh1, w_q, w_rest,
            b_in_l.reshape(1, IN_WIDTH), seq, (w_out[l], w_ff1[l], w_ff2[l]))
        attn = _attention(q, kv, bias_rows, attn_sinks[l], seq)
        zg = _ssm(u, prep, ssm_d[l], w_glu[l].astype(BF16), b_glu[l], bsz, seq)
        x2 = _merge(attn, zg, ga, gs, x2, g1, w_ap_l.astype(BF16),
                    w_ssm_proj[l].astype(BF16), w_out_b, seq, tm=512)
        x2 = _ffn(x2, norm2_g[l].reshape(1, D_MODEL), sc2, sh2, g2,
                  final_g.reshape(1, D_MODEL), w_ff1_b, w_ff2_b,
                  seq, tm=512, tf=1024)
    return x2.reshape(bsz, seq, D_MODEL)
```
